```python
import math
import jax, jax.numpy as jnp
from jax import lax
import numpy as np

D_MODEL = 4096
BATCH = 2
SEQ = 8192
DEPTH = 2

HEAD_DIM = 128
D_FF = 11008
ALPHA = (2 * DEPTH) ** 0.25
BETA = (8 * DEPTH) ** -0.25
LN_EPS = 1e-5
NEG_INF = -1e30

CONV_A_CH = 3072
CONV_A_WIDTH = 31
DSA_HEADS = 8
KV_RANK = 512
IDX_HEADS = 32
IDX_DIM = 128
IDX_TOPK = 256
Q_BLOCK = 128
CONV_C_CH = 3072
CONV_C_WIDTH = 3
DIL_SLOTS = 8
DIL_CONFIGS = ((128, 1), (512, 4), (2048, 16))
DIL_GROUPS = len(DIL_CONFIGS)
DIL_HEADS = DIL_SLOTS * DIL_GROUPS
DIL_BLOCK = 128
REL_BUCKETS = 32
REL_MAX_DIST = 128
REL_HEADS = 8

IN_E = 2 * CONV_A_CH + DSA_HEADS * HEAD_DIM + KV_RANK + IDX_HEADS * IDX_DIM + IDX_DIM + IDX_HEADS
OUT_E = CONV_A_CH + DSA_HEADS * HEAD_DIM
IN_O = 3 * CONV_C_CH + 3 * DIL_HEADS * HEAD_DIM
OUT_O = CONV_C_CH + DIL_SLOTS * HEAD_DIM
N_EVEN = (DEPTH + 1) // 2
N_ODD = DEPTH // 2

kernel_name = "hybrid_conformer_dsa_shortconv_dilated_trunk"


def layer_norm(x, g, b):
    xf = x.astype(jnp.float32)
    mu = jnp.mean(xf, axis=-1, keepdims=True)
    var = jnp.mean(jnp.square(xf - mu), axis=-1, keepdims=True)
    y = (xf - mu) * lax.rsqrt(var + LN_EPS)
    return (y * g.astype(jnp.float32) + b.astype(jnp.float32)).astype(x.dtype)


def rms_norm(x, g):
    xf = x.astype(jnp.float32)
    y = xf * lax.rsqrt(jnp.mean(jnp.square(xf), axis=-1, keepdims=True) + LN_EPS)
    return (y * g.astype(jnp.float32)).astype(x.dtype)


def swiglu(x, w_in, w_out):
    gate, up = jnp.split(x @ w_in, 2, axis=-1)
    return (jax.nn.silu(gate) * up) @ w_out


def split_cols(u, sizes):
    offs = np.cumsum([0] + list(sizes))
    return [u[..., int(offs[i]):int(offs[i + 1])] for i in range(len(sizes))]


def causal_dwconv(x, w):
    width, ch = w.shape
    return lax.conv_general_dilated(
        x, w[:, None, :].astype(x.dtype), window_strides=(1,), padding=[(width - 1, 0)],
        dimension_numbers=("NWC", "WIO", "NWC"), feature_group_count=ch)


def t5_bucket(dist):
    n = jnp.maximum(dist, 0)
    max_exact = REL_BUCKETS // 2
    nf = jnp.maximum(n, 1).astype(jnp.float32)
    large = max_exact + (jnp.log(nf / max_exact) / math.log(REL_MAX_DIST / max_exact)
                         * (REL_BUCKETS - max_exact)).astype(jnp.int32)
    large = jnp.minimum(large, REL_BUCKETS - 1)
    return jnp.where(n < max_exact, n, large)


def dsa_attention(q, c_kv, q_idx, k_idx, w_idx, w_uk, w_uv, rel_bias):
    B, S, H, Dh = q.shape
    n_keep = min(IDX_TOPK, S // 4)
    nb = S // Q_BLOCK
    q_lat = jnp.einsum("bshd,rhd->bshr", q, w_uk) * (Dh ** -0.5)
    w_idx = w_idx * (IDX_HEADS ** -0.5 * IDX_DIM ** -0.5)
    key_pos = jnp.arange(S)
    gather = jax.vmap(lambda table, idx: table[idx])

    def to_blocks(a):
        return jnp.moveaxis(a.reshape(B, nb, Q_BLOCK, *a.shape[2:]), 1, 0)

    def block(args):
        n, ql, qi, wi = args
        t = n * Q_BLOCK + jnp.arange(Q_BLOCK)
        rel = jax.nn.relu(jnp.einsum("bqhd,bsd->bqhs", qi, k_idx))
        score = jnp.einsum("bqhs,bqh->bqs", rel, wi).astype(jnp.float32)
        score = jnp.where(key_pos[None, None, :] <= t[None, :, None], score, NEG_INF)
        _, sel = lax.top_k(score, n_keep)
        dist = t[None, :, None] - sel
        c_sel = gather(c_kv, sel)
        logits = jnp.einsum("bqhr,bqkr->bqhk", ql, c_sel).astype(jnp.float32)
        bias = rel_bias[t5_bucket(dist)].astype(jnp.float32)
        logits = logits + jnp.moveaxis(bias, -1, 2)
        logits = jnp.where((dist >= 0)[:, :, None, :], logits, NEG_INF)
        p = jax.nn.softmax(logits, axis=-1).astype(c_kv.dtype)
        return jnp.einsum("bqhk,bqkr->bqhr", p, c_sel)

    o_lat = lax.map(block, (jnp.arange(nb), to_blocks(q_lat), to_blocks(q_idx), to_blocks(w_idx)))
    o_lat = jnp.moveaxis(o_lat, 0, 1).reshape(B, S, H, KV_RANK)
    return jnp.einsum("bshr,rhd->bshd", o_lat, w_uv)


def dilated_group(q, k, v, window, dilation, rel_bias):
    B, S, H, Dh = q.shape
    L = S // dilation
    nb = -(-L // DIL_BLOCK)
    Lp = nb * DIL_BLOCK
    steps = window // dilation

    def split(a):
        a = a.reshape(B, L, dilation, H, Dh).transpose(0, 2, 1, 3, 4)
        a = jnp.pad(a, ((0, 0), (0, 0), (0, Lp - L), (0, 0), (0, 0)))
        return a.reshape(B, dilation, nb, DIL_BLOCK, H, Dh)

    def band(a):
        prev = jnp.pad(a, ((0, 0), (0, 0), (1, 0), (0, 0), (0, 0), (0, 0)))[:, :, :-1]
        return jnp.concatenate([prev, a], axis=3)

    qb = split(q)
    kband, vband = band(split(k)), band(split(v))
    i = jnp.arange(DIL_BLOCK)[:, None]
    j = jnp.arange(2 * DIL_BLOCK)[None, :]
    m = DIL_BLOCK + i - j
    blk = jnp.arange(nb)[:, None, None]
    valid = (m >= 0) & (m <= steps) & ((blk - 1) * DIL_BLOCK + j >= 0)
    bias = rel_bias[t5_bucket(m * dilation)].astype(jnp.float32).transpose(2, 0, 1)
    logits = jnp.einsum("brnqhd,brnkhd->brnhqk", qb, kband).astype(jnp.float32) * (Dh ** -0.5) + bias
    logits = jnp.where(valid[None, None, :, None], logits, NEG_INF)
    lse = jax.nn.logsumexp(logits, axis=-1)
    p = jnp.exp(logits - lse[..., None]).astype(v.dtype)
    o = jnp.einsum("brnhqk,brnkhd->brnqhd", p, vband)

    def merge(a):
        a = a.reshape(B, dilation, Lp, *a.shape[4:])[:, :, :L]
        a = jnp.swapaxes(a, 1, 2)
        return a.reshape(B, S, *a.shape[3:])

    return merge(o), merge(jnp.swapaxes(lse, -1, -2))


def dilated_attention(q, k, v, rel_bias):
    outs, lses = [], []
    for g, (window, dilation) in enumerate(DIL_CONFIGS):
        o, lse = dilated_group(q[:, :, g], k[:, :, g], v[:, :, g], window, dilation, rel_bias)
        outs.append(o)
        lses.append(lse)
    wts = jax.nn.softmax(jnp.stack(lses, axis=0), axis=0).astype(v.dtype)
    return jnp.einsum("gbsh,gbshd->bshd", wts, jnp.stack(outs, axis=0))


def mixer_even(x, w_in, w_out, conv_w, conv_b, norm_g, norm_b, kv_norm_g, w_uk, w_uv, rel_bias):
    B, S, _ = x.shape
    u = x @ w_in
    a_val, a_gate, q, c_kv, q_idx, k_idx, w_idx = split_cols(
        u, [CONV_A_CH, CONV_A_CH, DSA_HEADS * HEAD_DIM, KV_RANK, IDX_HEADS * IDX_DIM, IDX_DIM, IDX_HEADS])
    a = a_val * jax.nn.sigmoid(a_gate)
    a = causal_dwconv(a, conv_w) + conv_b
    a = jax.nn.silu(layer_norm(a, norm_g, norm_b))
    o = dsa_attention(q.reshape(B, S, DSA_HEADS, HEAD_DIM), rms_norm(c_kv, kv_norm_g),
                      q_idx.reshape(B, S, IDX_HEADS, IDX_DIM), k_idx, w_idx, w_uk, w_uv, rel_bias)
    return jnp.concatenate([a, o.reshape(B, S, DSA_HEADS * HEAD_DIM)], axis=-1) @ w_out


def mixer_odd(x, w_in, w_out, conv_w, rel_bias):
    B, S, _ = x.shape
    u = x @ w_in
    gb, gc, h, q, k, v = split_cols(u, [CONV_C_CH] * 3 + [DIL_HEADS * HEAD_DIM] * 3)
    c_out = gb * causal_dwconv(gc * h, conv_w)
    shp = (B, S, DIL_GROUPS, DIL_SLOTS, HEAD_DIM)
    o = dilated_attention(q.reshape(shp), k.reshape(shp), v.reshape(shp), rel_bias)
    return jnp.concatenate([c_out, o.reshape(B, S, DIL_SLOTS * HEAD_DIM)], axis=-1) @ w_out


def setup_inputs(seed: int = 0) -> dict:
    key = jax.random.key(seed)
    ks = jax.random.split(key, 18)

    def nrm(k, shape, scale):
        return jax.random.normal(k, shape, jnp.float32) * scale

    return {
        "x": nrm(ks[0], (BATCH, SEQ, D_MODEL), 1.0),
        "w_ffn_in": nrm(ks[1], (DEPTH, 2, D_MODEL, 2 * D_FF), D_MODEL ** -0.5),
        "w_ffn_out": nrm(ks[2], (DEPTH, 2, D_FF, D_MODEL), BETA * D_FF ** -0.5),
        "post_ln_g": 1.0 + nrm(ks[3], (DEPTH, 3, D_MODEL), 0.02),
        "post_ln_b": nrm(ks[4], (DEPTH, 3, D_MODEL), 0.02),
        "rel_bias": nrm(ks[5], (REL_BUCKETS, REL_HEADS), 0.5),
        "w_in_e": nrm(ks[6], (N_EVEN, D_MODEL, IN_E), D_MODEL ** -0.5),
        "w_out_e": nrm(ks[7], (N_EVEN, OUT_E, D_MODEL), BETA * OUT_E ** -0.5),
        "conv_a_w": nrm(ks[8], (N_EVEN, CONV_A_WIDTH, CONV_A_CH), CONV_A_WIDTH ** -0.5),
        "conv_a_b": nrm(ks[9], (N_EVEN, CONV_A_CH), 0.02),
        "conv_a_norm_g": 1.0 + nrm(ks[10], (N_EVEN, CONV_A_CH), 0.02),
        "conv_a_norm_b": nrm(ks[11], (N_EVEN, CONV_A_CH), 0.02),
        "kv_norm_g": 1.0 + nrm(ks[12], (N_EVEN, KV_RANK), 0.02),
        "w_uk": nrm(ks[13], (N_EVEN, KV_RANK, DSA_HEADS, HEAD_DIM), KV_RANK ** -0.5),
        "w_uv": nrm(ks[14], (N_EVEN, KV_RANK, DSA_HEADS, HEAD_DIM), KV_RANK ** -0.5),
        "w_in_o": nrm(ks[15], (N_ODD, D_MODEL, IN_O), D_MODEL ** -0.5),
        "w_out_o": nrm(ks[16], (N_ODD, OUT_O, D_MODEL), BETA * OUT_O ** -0.5),
        "conv_c_w": nrm(ks[17], (N_ODD, CONV_C_WIDTH, CONV_C_CH), CONV_C_WIDTH ** -0.5),
    }


def reference(x, w_ffn_in, w_ffn_out, post_ln_g, post_ln_b, rel_bias, w_in_e, w_out_e,
              conv_a_w, conv_a_b, conv_a_norm_g, conv_a_norm_b, kv_norm_g, w_uk, w_uv,
              w_in_o, w_out_o, conv_c_w):
    for layer in range(DEPTH):
        g, b = post_ln_g[layer], post_ln_b[layer]
        x = layer_norm(ALPHA * x + 0.5 * swiglu(x, w_ffn_in[layer, 0], w_ffn_out[layer, 0]), g[0], b[0])
        if layer % 2 == 0:
            e = layer // 2
            mix = mixer_even(x, w_in_e[e], w_out_e[e], conv_a_w[e], conv_a_b[e], conv_a_norm_g[e],
                             conv_a_norm_b[e], kv_norm_g[e], w_uk[e], w_uv[e], rel_bias)
        else:
            o = layer // 2
            mix = mixer_odd(x, w_in_o[o], w_out_o[o], conv_c_w[o], rel_bias)
        x = layer_norm(ALPHA * x + mix, g[1], b[1])
        x = layer_norm(ALPHA * x + 0.5 * swiglu(x, w_ffn_in[layer, 1], w_ffn_out[layer, 1]), g[2], b[2])
    return x
```

```python
import functools
import math

import jax
import jax.numpy as jnp
import numpy as np
from jax import lax
from jax.experimental import pallas as pl
from jax.experimental.pallas import tpu as pltpu

DEPTH = 2
ALPHA = (2 * DEPTH) ** 0.25
LN_EPS = 1e-5
NEG_INF = -1e30

HEAD_DIM = 128
DSA_HEADS = 8
KV_RANK = 512
IDX_HEADS = 32
IDX_DIM = 128
IDX_TOPK = 256
Q_BLOCK = 128
DIL_SLOTS = 8
DIL_CONFIGS = ((128, 1), (512, 4), (2048, 16))
DIL_GROUPS = len(DIL_CONFIGS)
DIL_HEADS = DIL_SLOTS * DIL_GROUPS
DIL_BLOCK = 128
REL_BUCKETS = 32
REL_MAX_DIST = 128

V7X_VMEM_BYTES = 64 * 1024 * 1024
VMEM_LIMIT_BYTES = V7X_VMEM_BYTES - 6 * 1024 * 1024

BF16 = jnp.bfloat16
F32 = jnp.float32


def _layer_norm_rows(y, g, b):
    mu = jnp.mean(y, axis=-1, keepdims=True)
    yc = y - mu
    var = jnp.mean(yc * yc, axis=-1, keepdims=True)
    return yc * lax.rsqrt(var + LN_EPS) * g + b


FFN_ROWS = 512
FFN_HIDDEN = 256
LN_CHUNK = 32


def _ffn_ln_kernel(x_ref, wg_ref, wu_ref, wo_ref, g_ref, b_ref, o_ref, xb_ref):
    f = pl.program_id(1)

    @pl.when(f == 0)
    def _():
        xb_ref[...] = x_ref[...].astype(BF16)
        o_ref[...] = jnp.zeros_like(o_ref)

    xb = xb_ref[...]
    gate = jnp.dot(xb, wg_ref[...], preferred_element_type=F32)
    up = jnp.dot(xb, wu_ref[...], preferred_element_type=F32)
    h = (gate * jax.nn.sigmoid(gate) * up).astype(BF16)
    o_ref[...] += jnp.dot(h, wo_ref[...], preferred_element_type=F32)

    @pl.when(f == pl.num_programs(1) - 1)
    def _():
        g = g_ref[...]
        b = b_ref[...]

        def body(i, carry):
            rows = pl.ds(pl.multiple_of(i * LN_CHUNK, LN_CHUNK), LN_CHUNK)
            y = ALPHA * x_ref[rows, :] + 0.5 * o_ref[rows, :]
            o_ref[rows, :] = _layer_norm_rows(y, g, b)
            return carry

        lax.fori_loop(0, x_ref.shape[0] // LN_CHUNK, body, 0)


def ffn_ln(x, w_in, w_out, g, b):
    m, d = x.shape
    f = w_out.shape[0]
    tm = min(FFN_ROWS, m)
    tf = FFN_HIDDEN
    assert m % tm == 0 and f % tf == 0
    nf = f // tf
    return pl.pallas_call(
        _ffn_ln_kernel,
        grid=(m // tm, nf),
        in_specs=[
            pl.BlockSpec((tm, d), lambda i, j: (i, 0)),
            pl.BlockSpec((d, tf), lambda i, j: (0, j)),
            pl.BlockSpec((d, tf), lambda i, j: (0, j + nf)),
            pl.BlockSpec((tf, d), lambda i, j: (j, 0)),
            pl.BlockSpec((1, d), lambda i, j: (0, 0)),
            pl.BlockSpec((1, d), lambda i, j: (0, 0)),
        ],
        out_specs=pl.BlockSpec((tm, d), lambda i, j: (i, 0)),
        out_shape=jax.ShapeDtypeStruct((m, d), F32),
        scratch_shapes=[pltpu.VMEM((tm, d), BF16)],
        compiler_params=pltpu.CompilerParams(
            dimension_semantics=("parallel", "arbitrary"),
            vmem_limit_bytes=VMEM_LIMIT_BYTES),
        name="ffn_ln",
    )(x, w_in, w_in, w_out, g, b)


PROJ_ROWS = 1024
PROJ_COLS = 512


def _proj_kernel(x_ref, w_ref, o_ref, xb_ref):
    @pl.when(pl.program_id(1) == 0)
    def _():
        xb_ref[...] = x_ref[...].astype(BF16)

    o_ref[...] = jnp.dot(xb_ref[...], w_ref[...], preferred_element_type=F32).astype(o_ref.dtype)


def project(x, w, out_dtype=BF16):
    m, k = x.shape
    n = w.shape[1]
    tm = min(PROJ_ROWS, m)
    tn = min(PROJ_COLS, n)
    assert m % tm == 0 and n % tn == 0
    return pl.pallas_call(
        _proj_kernel,
        grid=(m // tm, n // tn),
        in_specs=[
            pl.BlockSpec((tm, k), lambda i, j: (i, 0)),
            pl.BlockSpec((k, tn), lambda i, j: (0, j)),
        ],
        out_specs=pl.BlockSpec((tm, tn), lambda i, j: (i, j)),
        out_shape=jax.ShapeDtypeStruct((m, n), out_dtype),
        scratch_shapes=[pltpu.VMEM((tm, k), BF16)],
        compiler_params=pltpu.CompilerParams(
            dimension_semantics=("parallel", "arbitrary"),
            vmem_limit_bytes=VMEM_LIMIT_BYTES),
        name="project",
    )(x, w)


OUT_ROWS = 512
OUT_K = 512


def _out_ln_kernel(a_ref, w_ref, x_ref, g_ref, b_ref, o_ref):
    kk = pl.program_id(1)

    @pl.when(kk == 0)
    def _():
        o_ref[...] = jnp.zeros_like(o_ref)

    o_ref[...] += jnp.dot(a_ref[...], w_ref[...], preferred_element_type=F32)

    @pl.when(kk == pl.num_programs(1) - 1)
    def _():
        g = g_ref[...]
        b = b_ref[...]

        def body(i, carry):
            rows = pl.ds(pl.multiple_of(i * LN_CHUNK, LN_CHUNK), LN_CHUNK)
            y = ALPHA * x_ref[rows, :] + o_ref[rows, :]
            o_ref[rows, :] = _layer_norm_rows(y, g, b)
            return carry

        lax.fori_loop(0, x_ref.shape[0] // LN_CHUNK, body, 0)


def out_ln(a, w, x, g, b):
    m, k = a.shape
    d = w.shape[1]
    tm = min(OUT_ROWS, m)
    tk = min(OUT_K, k)
    assert m % tm == 0 and k % tk == 0
    return pl.pallas_call(
        _out_ln_kernel,
        grid=(m // tm, k // tk),
        in_specs=[
            pl.BlockSpec((tm, tk), lambda i, j: (i, j)),
            pl.BlockSpec((tk, d), lambda i, j: (j, 0)),
            pl.BlockSpec((tm, d), lambda i, j: (i, 0)),
            pl.BlockSpec((1, d), lambda i, j: (0, 0)),
            pl.BlockSpec((1, d), lambda i, j: (0, 0)),
        ],
        out_specs=pl.BlockSpec((tm, d), lambda i, j: (i, 0)),
        out_shape=jax.ShapeDtypeStruct((m, d), F32),
        compiler_params=pltpu.CompilerParams(
            dimension_semantics=("parallel", "arbitrary"),
            vmem_limit_bytes=VMEM_LIMIT_BYTES),
        name="out_ln",
    )(a, w, x, g, b)


def _layer_norm(x, g, b):
    mu = jnp.mean(x, axis=-1, keepdims=True)
    var = jnp.mean(jnp.square(x - mu), axis=-1, keepdims=True)
    return (x - mu) * lax.rsqrt(var + LN_EPS) * g + b


def _rms_norm(x, g):
    return x * lax.rsqrt(jnp.mean(jnp.square(x), axis=-1, keepdims=True) + LN_EPS) * g


def _split_cols(u, sizes):
    offs = np.cumsum([0] + list(sizes))
    return [u[..., int(offs[i]):int(offs[i + 1])] for i in range(len(sizes))]


def _causal_dwconv(x, w):
    width, ch = w.shape
    return lax.conv_general_dilated(
        x, w[:, None, :].astype(x.dtype), window_strides=(1,), padding=[(width - 1, 0)],
        dimension_numbers=("NWC", "WIO", "NWC"), feature_group_count=ch)


def _t5_bucket(dist):
    n = jnp.maximum(dist, 0)
    max_exact = REL_BUCKETS // 2
    nf = jnp.maximum(n, 1).astype(jnp.float32)
    large = max_exact + (jnp.log(nf / max_exact) / math.log(REL_MAX_DIST / max_exact)
                         * (REL_BUCKETS - max_exact)).astype(jnp.int32)
    large = jnp.minimum(large, REL_BUCKETS - 1)
    return jnp.where(n < max_exact, n, large)


def _dsa_attention(q, c_kv, q_idx, k_idx, w_idx, w_uk, w_uv, rel_bias):
    B, S, H, Dh = q.shape
    n_keep = min(IDX_TOPK, S // 4)
    nb = S // Q_BLOCK
    q_lat = jnp.einsum("bshd,rhd->bshr", q, w_uk) * (Dh ** -0.5)
    w_idx = w_idx * (IDX_HEADS ** -0.5 * IDX_DIM ** -0.5)
    key_pos = jnp.arange(S)
    gather = jax.vmap(lambda table, idx: table[idx])

    def to_blocks(a):
        return jnp.moveaxis(a.reshape(B, nb, Q_BLOCK, *a.shape[2:]), 1, 0)

    def block(args):
        n, ql, qi, wi = args
        t = n * Q_BLOCK + jnp.arange(Q_BLOCK)
        rel = jax.nn.relu(jnp.einsum("bqhd,bsd->bqhs", qi, k_idx))
        score = jnp.einsum("bqhs,bqh->bqs", rel, wi).astype(jnp.float32)
        score = jnp.where(key_pos[None, None, :] <= t[None, :, None], score, NEG_INF)
        _, sel = lax.top_k(score, n_keep)
        dist = t[None, :, None] - sel
        c_sel = gather(c_kv, sel)
        logits = jnp.einsum("bqhr,bqkr->bqhk", ql, c_sel).astype(jnp.float32)
        bias = rel_bias[_t5_bucket(dist)].astype(jnp.float32)
        logits = logits + jnp.moveaxis(bias, -1, 2)
        logits = jnp.where((dist >= 0)[:, :, None, :], logits, NEG_INF)
        p = jax.nn.softmax(logits, axis=-1).astype(c_kv.dtype)
        return jnp.einsum("bqhk,bqkr->bqhr", p, c_sel)

    o_lat = lax.map(block, (jnp.arange(nb), to_blocks(q_lat), to_blocks(q_idx), to_blocks(w_idx)))
    o_lat = jnp.moveaxis(o_lat, 0, 1).reshape(B, S, H, KV_RANK)
    return jnp.einsum("bshr,rhd->bshd", o_lat, w_uv)


def _dilated_group(q, k, v, window, dilation, rel_bias):
    B, S, H, Dh = q.shape
    L = S // dilation
    nb = -(-L // DIL_BLOCK)
    Lp = nb * DIL_BLOCK
    steps = window // dilation

    def split(a):
        a = a.reshape(B, L, dilation, H, Dh).transpose(0, 2, 1, 3, 4)
        a = jnp.pad(a, ((0, 0), (0, 0), (0, Lp - L), (0, 0), (0, 0)))
        return a.reshape(B, dilation, nb, DIL_BLOCK, H, Dh)

    def band(a):
        prev = jnp.pad(a, ((0, 0), (0, 0), (1, 0), (0, 0), (0, 0), (0, 0)))[:, :, :-1]
        return jnp.concatenate([prev, a], axis=3)

    qb = split(q)
    kband, vband = band(split(k)), band(split(v))
    i = jnp.arange(DIL_BLOCK)[:, None]
    j = jnp.arange(2 * DIL_BLOCK)[None, :]
    m = DIL_BLOCK + i - j
    blk = jnp.arange(nb)[:, None, None]
    valid = (m >= 0) & (m <= steps) & ((blk - 1) * DIL_BLOCK + j >= 0)
    bias = rel_bias[_t5_bucket(m * dilation)].astype(jnp.float32).transpose(2, 0, 1)
    logits = jnp.einsum("brnqhd,brnkhd->brnhqk", qb, kband).astype(jnp.float32) * (Dh ** -0.5) + bias
    logits = jnp.where(valid[None, None, :, None], logits, NEG_INF)
    lse = jax.nn.logsumexp(logits, axis=-1)
    p = jnp.exp(logits - lse[..., None]).astype(v.dtype)
    o = jnp.einsum("brnhqk,brnkhd->brnqhd", p, vband)

    def merge(a):
        a = a.reshape(B, dilation, Lp, *a.shape[4:])[:, :, :L]
        a = jnp.swapaxes(a, 1, 2)
        return a.reshape(B, S, *a.shape[3:])

    return merge(o), merge(jnp.swapaxes(lse, -1, -2))


def _dilated_attention(q, k, v, rel_bias):
    outs, lses = [], []
    for g, (window, dilation) in enumerate(DIL_CONFIGS):
        o, lse = _dilated_group(q[:, :, g], k[:, :, g], v[:, :, g], window, dilation, rel_bias)
        outs.append(o)
        lses.append(lse)
    wts = jax.nn.softmax(jnp.stack(lses, axis=0), axis=0).astype(v.dtype)
    return jnp.einsum("gbsh,gbshd->bshd", wts, jnp.stack(outs, axis=0))


def _mixer_even_core(u, conv_w, conv_b, norm_g, norm_b, kv_norm_g, w_uk, w_uv, rel_bias):
    B, S, _ = u.shape
    ca = conv_w.shape[-1]
    a_val, a_gate, q, c_kv, q_idx, k_idx, w_idx = _split_cols(
        u, [ca, ca, DSA_HEADS * HEAD_DIM, KV_RANK, IDX_HEADS * IDX_DIM, IDX_DIM, IDX_HEADS])
    a = a_val * jax.nn.sigmoid(a_gate)
    a = _causal_dwconv(a, conv_w) + conv_b
    a = jax.nn.silu(_layer_norm(a, norm_g, norm_b))
    o = _dsa_attention(q.reshape(B, S, DSA_HEADS, HEAD_DIM), _rms_norm(c_kv, kv_norm_g),
                       q_idx.reshape(B, S, IDX_HEADS, IDX_DIM), k_idx, w_idx, w_uk, w_uv, rel_bias)
    return jnp.concatenate([a, o.reshape(B, S, DSA_HEADS * HEAD_DIM)], axis=-1)


def _mixer_odd_core(u, conv_w, rel_bias):
    B, S, _ = u.shape
    cc = conv_w.shape[-1]
    gb, gc, h, q, k, v = _split_cols(u, [cc] * 3 + [DIL_HEADS * HEAD_DIM] * 3)
    c_out = gb * _causal_dwconv(gc * h, conv_w)
    shp = (B, S, DIL_GROUPS, DIL_SLOTS, HEAD_DIM)
    o = _dilated_attention(q.reshape(shp), k.reshape(shp), v.reshape(shp), rel_bias)
    return jnp.concatenate([c_out, o.reshape(B, S, DIL_SLOTS * HEAD_DIM)], axis=-1)


def _pad_cols(w, mult):
    n = w.shape[-1]
    pad = (-n) % mult
    return jnp.pad(w, ((0, 0), (0, pad))) if pad else w


def kernel(x, w_ffn_in, w_ffn_out, post_ln_g, post_ln_b, rel_bias, w_in_e, w_out_e, conv_a_w, conv_a_b,
           conv_a_norm_g, conv_a_norm_b, kv_norm_g, w_uk, w_uv, w_in_o, w_out_o, conv_c_w):
    B, S, D = x.shape
    M = B * S
    depth = w_ffn_in.shape[0]
    xs = x.reshape(M, D)
    w_ffn_in_b = w_ffn_in.astype(BF16)
    w_ffn_out_b = w_ffn_out.astype(BF16)
    for layer in range(depth):
        g = post_ln_g[layer].reshape(3, 1, D)
        b = post_ln_b[layer].reshape(3, 1, D)
        xs = ffn_ln(xs, w_ffn_in_b[layer, 0], w_ffn_out_b[layer, 0], g[0], b[0])
        if layer % 2 == 0:
            e = layer // 2
            n_in = w_in_e.shape[-1]
            u = project(xs, _pad_cols(w_in_e[e].astype(BF16), PROJ_COLS))[:, :n_in]
            mix = _mixer_even_core(u.astype(F32).reshape(B, S, n_in), conv_a_w[e], conv_a_b[e], conv_a_norm_g[e],
                                   conv_a_norm_b[e], kv_norm_g[e], w_uk[e], w_uv[e], rel_bias)
            w_out = w_out_e[e]
        else:
            o = layer // 2
            n_in = w_in_o.shape[-1]
            u = project(xs, _pad_cols(w_in_o[o].astype(BF16), PROJ_COLS))[:, :n_in]
            mix = _mixer_odd_core(u.astype(F32).reshape(B, S, n_in), conv_c_w[o], rel_bias)
            w_out = w_out_o[o]
        xs = out_ln(mix.reshape(M, -1).astype(BF16), w_out.astype(BF16), xs, g[1], b[1])
        xs = ffn_ln(xs, w_ffn_in_b[layer, 1], w_ffn_out_b[layer, 1], g[2], b[2])
    return xs.reshape(B, S, D)
```

```python
import functools
import math

import jax
import jax.numpy as jnp
import numpy as np
from jax import lax
from jax.experimental import pallas as pl
from jax.experimental.pallas import tpu as pltpu

DEPTH = 2
ALPHA = (2 * DEPTH) ** 0.25
LN_EPS = 1e-5
NEG_INF = -1e30

HEAD_DIM = 128
DSA_HEADS = 8
KV_RANK = 512
IDX_HEADS = 32
IDX_DIM = 128
IDX_TOPK = 256
DIL_SLOTS = 8
DIL_CONFIGS = ((128, 1), (512, 4), (2048, 16))
DIL_GROUPS = len(DIL_CONFIGS)
DIL_HEADS = DIL_SLOTS * DIL_GROUPS
DIL_BLOCK = 128
REL_BUCKETS = 32
REL_MAX_DIST = 128

V7X_VMEM_BYTES = 64 * 1024 * 1024
VMEM_LIMIT_BYTES = V7X_VMEM_BYTES - 6 * 1024 * 1024

BF16 = jnp.bfloat16
F32 = jnp.float32
I32 = jnp.int32
INT_MIN = -2 ** 31


def _layer_norm_rows(y, g, b):
    mu = jnp.mean(y, axis=-1, keepdims=True)
    yc = y - mu
    var = jnp.mean(yc * yc, axis=-1, keepdims=True)
    return yc * lax.rsqrt(var + LN_EPS) * g + b


def _nt_dot(a, b):
    return lax.dot_general(a, b, (((1,), (1,)), ((), ())), preferred_element_type=F32)


def _split_cols(u, sizes):
    offs = np.cumsum([0] + list(sizes))
    return [u[..., int(offs[i]):int(offs[i + 1])] for i in range(len(sizes))]


FFN_ROWS = 512
FFN_HIDDEN = 256
LN_CHUNK = 32


def _ffn_ln_kernel(x_ref, wg_ref, wu_ref, wo_ref, g_ref, b_ref, o_ref, xb_ref):
    f = pl.program_id(1)

    @pl.when(f == 0)
    def _():
        xb_ref[...] = x_ref[...].astype(BF16)
        o_ref[...] = jnp.zeros_like(o_ref)

    xb = xb_ref[...]
    gate = jnp.dot(xb, wg_ref[...], preferred_element_type=F32)
    up = jnp.dot(xb, wu_ref[...], preferred_element_type=F32)
    h = (gate * jax.nn.sigmoid(gate) * up).astype(BF16)
    o_ref[...] += jnp.dot(h, wo_ref[...], preferred_element_type=F32)

    @pl.when(f == pl.num_programs(1) - 1)
    def _():
        g = g_ref[...]
        b = b_ref[...]

        def body(i, carry):
            rows = pl.ds(pl.multiple_of(i * LN_CHUNK, LN_CHUNK), LN_CHUNK)
            y = ALPHA * x_ref[rows, :] + 0.5 * o_ref[rows, :]
            o_ref[rows, :] = _layer_norm_rows(y, g, b)
            return carry

        lax.fori_loop(0, x_ref.shape[0] // LN_CHUNK, body, 0)


def ffn_ln(x, w_in, w_out, g, b):
    m, d = x.shape
    f = w_out.shape[0]
    tm = min(FFN_ROWS, m)
    tf = FFN_HIDDEN
    assert m % tm == 0 and f % tf == 0
    nf = f // tf
    return pl.pallas_call(
        _ffn_ln_kernel,
        grid=(m // tm, nf),
        in_specs=[
            pl.BlockSpec((tm, d), lambda i, j: (i, 0)),
            pl.BlockSpec((d, tf), lambda i, j: (0, j)),
            pl.BlockSpec((d, tf), lambda i, j: (0, j + nf)),
            pl.BlockSpec((tf, d), lambda i, j: (j, 0)),
            pl.BlockSpec((1, d), lambda i, j: (0, 0)),
            pl.BlockSpec((1, d), lambda i, j: (0, 0)),
        ],
        out_specs=pl.BlockSpec((tm, d), lambda i, j: (i, 0)),
        out_shape=jax.ShapeDtypeStruct((m, d), F32),
        scratch_shapes=[pltpu.VMEM((tm, d), BF16)],
        compiler_params=pltpu.CompilerParams(
            dimension_semantics=("parallel", "arbitrary"),
            vmem_limit_bytes=VMEM_LIMIT_BYTES),
        name="ffn_ln",
    )(x, w_in, w_in, w_out, g, b)


PROJ_ROWS = 1024
PROJ_COLS = 512


def _proj_kernel(x_ref, w_ref, o_ref, xb_ref):
    @pl.when(pl.program_id(1) == 0)
    def _():
        xb_ref[...] = x_ref[...].astype(BF16)

    o_ref[...] = jnp.dot(xb_ref[...], w_ref[...], preferred_element_type=F32).astype(o_ref.dtype)


def project(x, w):
    m, k = x.shape
    n = w.shape[1]
    tm = min(PROJ_ROWS, m)
    tn = min(PROJ_COLS, n)
    assert m % tm == 0 and n % tn == 0
    return pl.pallas_call(
        _proj_kernel,
        grid=(m // tm, n // tn),
        in_specs=[
            pl.BlockSpec((tm, k), lambda i, j: (i, 0)),
            pl.BlockSpec((k, tn), lambda i, j: (0, j)),
        ],
        out_specs=pl.BlockSpec((tm, tn), lambda i, j: (i, j)),
        out_shape=jax.ShapeDtypeStruct((m, n), BF16),
        scratch_shapes=[pltpu.VMEM((tm, k), BF16)],
        compiler_params=pltpu.CompilerParams(
            dimension_semantics=("parallel", "arbitrary"),
            vmem_limit_bytes=VMEM_LIMIT_BYTES),
        name="project",
    )(x, w)


OUT_ROWS = 512
OUT_K = 512


def _out_ln_kernel(a_ref, w_ref, x_ref, g_ref, b_ref, o_ref):
    kk = pl.program_id(1)

    @pl.when(kk == 0)
    def _():
        o_ref[...] = jnp.zeros_like(o_ref)

    o_ref[...] += jnp.dot(a_ref[...], w_ref[...], preferred_element_type=F32)

    @pl.when(kk == pl.num_programs(1) - 1)
    def _():
        g = g_ref[...]
        b = b_ref[...]

        def body(i, carry):
            rows = pl.ds(pl.multiple_of(i * LN_CHUNK, LN_CHUNK), LN_CHUNK)
            y = ALPHA * x_ref[rows, :] + o_ref[rows, :]
            o_ref[rows, :] = _layer_norm_rows(y, g, b)
            return carry

        lax.fori_loop(0, x_ref.shape[0] // LN_CHUNK, body, 0)


def out_ln(a, w, x, g, b):
    m, k = a.shape
    d = w.shape[1]
    tm = min(OUT_ROWS, m)
    tk = min(OUT_K, k)
    assert m % tm == 0 and k % tk == 0
    return pl.pallas_call(
        _out_ln_kernel,
        grid=(m // tm, k // tk),
        in_specs=[
            pl.BlockSpec((tm, tk), lambda i, j: (i, j)),
            pl.BlockSpec((tk, d), lambda i, j: (j, 0)),
            pl.BlockSpec((tm, d), lambda i, j: (i, 0)),
            pl.BlockSpec((1, d), lambda i, j: (0, 0)),
            pl.BlockSpec((1, d), lambda i, j: (0, 0)),
        ],
        out_specs=pl.BlockSpec((tm, d), lambda i, j: (i, 0)),
        out_shape=jax.ShapeDtypeStruct((m, d), F32),
        compiler_params=pltpu.CompilerParams(
            dimension_semantics=("parallel", "arbitrary"),
            vmem_limit_bytes=VMEM_LIMIT_BYTES),
        name="out_ln",
    )(a, w, x, g, b)


def _t5_thresholds():
    d = np.arange(0, 2 * REL_MAX_DIST)
    max_exact = REL_BUCKETS // 2
    nf = np.maximum(d, 1).astype(np.float32)
    large = max_exact + (np.log(nf / np.float32(max_exact)) / np.float32(math.log(REL_MAX_DIST / max_exact))
                         * np.float32(REL_BUCKETS - max_exact)).astype(np.int32)
    large = np.minimum(large, REL_BUCKETS - 1)
    bucket = np.where(d < max_exact, d, large)
    return [int(np.argmax(bucket >= k)) for k in range(REL_BUCKETS)]


T5_THR = _t5_thresholds()


def _t5_bias_tile(relb_ref, head, dist):
    acc = jnp.full(dist.shape, relb_ref[0, head], F32)
    for k in range(1, REL_BUCKETS):
        acc = jnp.where(dist >= T5_THR[k], relb_ref[k, head], acc)
    return acc


DSA_TQ = 128
DSA_TS = 512
SUB = 128
assert T5_THR[REL_BUCKETS - 1] <= SUB


def _ckv_kernel(c_ref, g_ref, cn_ref, ct_ref):
    c = c_ref[0].astype(F32)
    y = c * lax.rsqrt(jnp.mean(c * c, axis=-1, keepdims=True) + LN_EPS) * g_ref[...]
    cn_ref[0] = y.astype(BF16)
    ct_ref[0, 0] = y.T.astype(BF16)


def ckv_norm(u, g, col_block, ts):
    B, S, _ = u.shape
    R = KV_RANK
    return pl.pallas_call(
        _ckv_kernel,
        grid=(B, S // ts),
        in_specs=[pl.BlockSpec((1, ts, R), lambda b, j: (b, j, col_block)),
                  pl.BlockSpec((1, R), lambda b, j: (0, 0))],
        out_specs=[pl.BlockSpec((1, ts, R), lambda b, j: (b, j, 0)),
                   pl.BlockSpec((1, 1, R, ts), lambda b, j: (b, j, 0, 0))],
        out_shape=[jax.ShapeDtypeStruct((B, S, R), BF16),
                   jax.ShapeDtypeStruct((B, S // ts, R, ts), BF16)],
        compiler_params=pltpu.CompilerParams(dimension_semantics=("parallel", "parallel")),
        name="ckv_norm",
    )(u, g)


def _dsa_kernel(relb_ref, qi_ref, q_ref, wi_ref, k_ref, c_ref, ct_ref, wuk_ref, wuvt_ref, o_ref,
                qm_ref, qlt_ref, wt_ref, key_ref, acc_ref, m_ref, l_ref, bias_ref, *, n_keep):
    b = pl.program_id(0)
    qt = pl.program_id(1)
    S = k_ref.shape[1]
    TS = ct_ref.shape[3]
    TQ = DSA_TQ
    H = DSA_HEADS
    n_sub = TS // SUB
    n_s = (qt * TQ + TQ + TS - 1) // TS
    idx_bits = int(S).bit_length()

    @pl.when((b == 0) & (qt == 0))
    def _():
        s_l = lax.broadcasted_iota(I32, (SUB, TQ), 0)
        t_l = lax.broadcasted_iota(I32, (SUB, TQ), 1)
        for which in range(2):
            for h in range(H):
                bias_ref[which, h] = _t5_bias_tile(relb_ref, h, t_l - s_l + SUB * which)

    for h in range(IDX_HEADS):
        qm_ref[h * TQ:(h + 1) * TQ, :] = qi_ref[0, :, h * IDX_DIM:(h + 1) * IDX_DIM]
    for h in range(H):
        ql = _nt_dot(wuk_ref[:, h * HEAD_DIM:(h + 1) * HEAD_DIM], q_ref[0, :, h * HEAD_DIM:(h + 1) * HEAD_DIM])
        qlt_ref[:, h * TQ:(h + 1) * TQ] = (ql * (HEAD_DIM ** -0.5)).astype(BF16)
    wt_ref[...] = wi_ref[0].astype(F32).T * (IDX_HEADS ** -0.5 * IDX_DIM ** -0.5)

    t_pos = qt * TQ + lax.broadcasted_iota(I32, (TS, TQ), 1)
    s_loc = lax.broadcasted_iota(I32, (TS, TQ), 0)

    def idx_body(j, carry):
        row0 = pl.multiple_of(j * TS, TS)
        kt = k_ref[0, pl.ds(row0, TS), :]
        score = jnp.zeros((TS, TQ), F32)
        for hp in range(IDX_HEADS // 2):
            sc = _nt_dot(kt, qm_ref[hp * 2 * TQ:(hp + 1) * 2 * TQ, :])
            for i in range(2):
                h = 2 * hp + i
                score = score + jnp.maximum(sc[:, i * TQ:(i + 1) * TQ], 0.0) * wt_ref[h:h + 1, :]
        bits = pltpu.bitcast(score, I32)
        key = bits ^ ((bits >> 31) & 0x7FFFFFFF)
        key = jnp.where(row0 + s_loc <= t_pos, key, INT_MIN)
        key_ref[pl.ds(row0, TS), :] = key
        return carry

    lax.fori_loop(0, n_s, idx_body, 0)

    def count(pred_fn):
        def body(j, part):
            row0 = pl.multiple_of(j * TS, TS)
            blk = key_ref[pl.ds(row0, TS), :]
            hit = pred_fn(blk, row0).astype(I32)
            return part + hit.reshape(TS // 8, 8, TQ).sum(axis=0)
        part = lax.fori_loop(0, n_s, body, jnp.zeros((8, TQ), I32))
        return part.sum(axis=0, keepdims=True)

    c0 = count(lambda blk, row0: blk >= 0)
    thr = jnp.where(c0 >= n_keep, jnp.zeros((1, TQ), I32), jnp.full((1, TQ), INT_MIN, I32))

    def bit_body(i, thr):
        cand = thr | (jnp.int32(1) << (30 - i))
        c = count(lambda blk, row0: blk >= cand)
        return jnp.where(c >= n_keep, cand, thr)

    thr = lax.fori_loop(0, 31, bit_body, thr)

    c_gt = count(lambda blk, row0: blk > thr)
    c_ge = count(lambda blk, row0: blk >= thr)
    need = n_keep - c_gt

    def tie_search():
        def tie_body(i, x):
            cand = x | (jnp.int32(1) << (idx_bits - 1 - i))
            c = count(lambda blk, row0: (blk == thr) & (row0 + s_loc < cand))
            return jnp.where(c < need, cand, x)
        return lax.fori_loop(0, idx_bits, tie_body, jnp.zeros((1, TQ), I32))

    tie_pos = lax.cond(jnp.max(c_ge) > n_keep, tie_search, lambda: jnp.full((1, TQ), 2 ** idx_bits - 1, I32))

    m_ref[...] = jnp.full(m_ref.shape, NEG_INF, F32)
    l_ref[...] = jnp.zeros(l_ref.shape, F32)
    acc_ref[...] = jnp.zeros(acc_ref.shape, F32)

    def att_body(j, carry):
        row0 = pl.multiple_of(j * TS, TS)
        key = key_ref[pl.ds(row0, TS), :]
        s_pos = row0 + s_loc
        sel = ((key > thr) | ((key == thr) & (s_pos <= tie_pos))) & (s_pos <= t_pos)
        ckv = c_ref[0, pl.ds(row0, TS), :]
        ckv_t = ct_ref[0, j]
        for hp in range(H // 2):
            lg = jnp.dot(ckv, qlt_ref[:, hp * 2 * TQ:(hp + 1) * 2 * TQ], preferred_element_type=F32)
            ps, alphas = [], []
            for i in range(2):
                h = 2 * hp + i
                cols = slice(h * TQ, (h + 1) * TQ)
                far = relb_ref[REL_BUCKETS - 1, h]
                pieces = []
                for sb in range(n_sub):
                    delta = qt - (j * n_sub + sb)
                    pieces.append(jnp.where(delta == 0, bias_ref[0, h],
                                            jnp.where(delta == 1, bias_ref[1, h], far)))
                bias = pieces[0] if n_sub == 1 else jnp.concatenate(pieces, axis=0)
                lh = jnp.where(sel, lg[:, i * TQ:(i + 1) * TQ] + bias, NEG_INF)
                m_old = m_ref[:, cols]
                m_new = jnp.maximum(m_old, jnp.max(lh, axis=0, keepdims=True))
                p = jnp.where(sel, jnp.exp(lh - m_new), 0.0)
                alpha = jnp.exp(m_old - m_new)
                l_ref[:, cols] = alpha * l_ref[:, cols] + jnp.sum(p, axis=0, keepdims=True)
                m_ref[:, cols] = m_new
                ps.append(p.astype(BF16))
                alphas.append(alpha)
            pv = jnp.dot(ckv_t, jnp.concatenate(ps, axis=1), preferred_element_type=F32)
            cols2 = slice(hp * 2 * TQ, (hp + 1) * 2 * TQ)
            acc_ref[:, cols2] = acc_ref[:, cols2] * jnp.concatenate(alphas, axis=1) + pv
        return carry

    lax.fori_loop(0, n_s, att_body, 0)

    for h in range(H):
        cols = slice(h * TQ, (h + 1) * TQ)
        ol_t = (acc_ref[:, cols] * (1.0 / l_ref[:, cols])).astype(BF16)
        o_t = jnp.dot(wuvt_ref[h], ol_t, preferred_element_type=F32)
        o_ref[0, :, h * HEAD_DIM:(h + 1) * HEAD_DIM] = o_t.T.astype(o_ref.dtype)


def dsa_attention(u, ckv_n, ckv_t, w_uk2, w_uvt, rel_bias, cols, n_keep):
    B, S, _ = u.shape
    ts = ckv_t.shape[3]
    H, R = DSA_HEADS, KV_RANK
    qi_blk, q_blk, wi_blk, k_blk = cols
    kern = functools.partial(_dsa_kernel, n_keep=n_keep)
    return pl.pallas_call(
        kern,
        grid=(B, S // DSA_TQ),
        in_specs=[
            pl.BlockSpec(memory_space=pltpu.SMEM),
            pl.BlockSpec((1, DSA_TQ, IDX_HEADS * IDX_DIM), lambda b, t: (b, t, qi_blk)),
            pl.BlockSpec((1, DSA_TQ, H * HEAD_DIM), lambda b, t: (b, t, q_blk)),
            pl.BlockSpec((1, DSA_TQ, 128), lambda b, t: (b, t, wi_blk)),
            pl.BlockSpec((1, S, IDX_DIM), lambda b, t: (b, 0, k_blk)),
            pl.BlockSpec((1, S, R), lambda b, t: (b, 0, 0)),
            pl.BlockSpec((1, S // ts, R, ts), lambda b, t: (b, 0, 0, 0)),
            pl.BlockSpec((R, H * HEAD_DIM), lambda b, t: (0, 0)),
            pl.BlockSpec((H, HEAD_DIM, R), lambda b, t: (0, 0, 0)),
        ],
        out_specs=pl.BlockSpec((1, DSA_TQ, H * HEAD_DIM), lambda b, t: (b, t, 0)),
        out_shape=jax.ShapeDtypeStruct((B, S, H * HEAD_DIM), BF16),
        scratch_shapes=[
            pltpu.VMEM((IDX_HEADS * DSA_TQ, IDX_DIM), BF16),
            pltpu.VMEM((R, H * DSA_TQ), BF16),
            pltpu.VMEM((128, DSA_TQ), F32),
            pltpu.VMEM((S, DSA_TQ), I32),
            pltpu.VMEM((R, H * DSA_TQ), F32),
            pltpu.VMEM((1, H * DSA_TQ), F32),
            pltpu.VMEM((1, H * DSA_TQ), F32),
            pltpu.VMEM((2, H, SUB, DSA_TQ), F32),
        ],
        compiler_params=pltpu.CompilerParams(
            dimension_semantics=("arbitrary", "arbitrary"),
            vmem_limit_bytes=VMEM_LIMIT_BYTES),
        name="dsa_attention",
    )(rel_bias, u, u, u, u, ckv_n, ckv_t, w_uk2, w_uvt)


SUBLANES = 8
CONV_ROWS = 256
CONV_HALO = 32
CONV_RC = 64
CONV_CW = 512


def _conv_taps(ext_ref, w_ref, y_ref, width, init_fn):
    ts, ch = y_ref.shape
    cw = min(CONV_CW, ch)
    n = CONV_RC + SUBLANES
    assert CONV_HALO >= SUBLANES * ((width - 1) // SUBLANES + 1)

    def body(i, carry):
        r0 = pl.multiple_of(i * CONV_RC, CONV_RC)
        for cc in range(ch // cw):
            cols = slice(cc * cw, (cc + 1) * cw)
            acc = init_fn(cols)
            for b in range(min(SUBLANES, width)):
                z = None
                for a in range((width - b + SUBLANES - 1) // SUBLANES):
                    j = SUBLANES * a + b
                    win = ext_ref[pl.ds(r0 + (CONV_HALO - SUBLANES * (a + 1)), n), cols]
                    term = win * w_ref[width - 1 - j:width - j, cols]
                    z = term if z is None else z + term
                if b == 0:
                    acc = acc + z[SUBLANES:, :]
                else:
                    acc = acc + pltpu.roll(z, n - (SUBLANES - b), axis=0)[:CONV_RC, :]
            y_ref[pl.ds(r0, CONV_RC), cols] = acc
        return carry

    lax.fori_loop(0, ts // CONV_RC, body, 0)


def _conv_gated_kernel(v_ref, g_ref, vp_ref, gp_ref, w_ref, cb_ref, ng_ref, nb_ref, o_ref, ext_ref, y_ref):
    t = pl.program_id(1)
    ts = v_ref.shape[1]
    width = w_ref.shape[0]
    prev = vp_ref[0].astype(F32) * jax.nn.sigmoid(gp_ref[0].astype(F32))
    ext_ref[0:CONV_HALO, :] = jnp.where(t > 0, prev, 0.0)

    def glu_body(i, carry):
        rows = pl.ds(pl.multiple_of(i * CONV_RC, CONV_RC), CONV_RC)
        ext_ref[pl.ds(pl.multiple_of(CONV_HALO + i * CONV_RC, CONV_HALO), CONV_RC), :] = (
            v_ref[0, rows, :].astype(F32) * jax.nn.sigmoid(g_ref[0, rows, :].astype(F32)))
        return carry

    lax.fori_loop(0, ts // CONV_RC, glu_body, 0)
    _conv_taps(ext_ref, w_ref, y_ref, width,
               lambda cols: jnp.broadcast_to(cb_ref[:, cols], (CONV_RC, cols.stop - cols.start)))
    g = ng_ref[...]
    b = nb_ref[...]

    def ln_body(i, carry):
        rows = pl.ds(pl.multiple_of(i * LN_CHUNK, LN_CHUNK), LN_CHUNK)
        z = _layer_norm_rows(y_ref[rows, :], g, b)
        o_ref[0, rows, :] = (z * jax.nn.sigmoid(z)).astype(o_ref.dtype)
        return carry

    lax.fori_loop(0, ts // LN_CHUNK, ln_body, 0)


def _conv_short_kernel(gb_ref, gc_ref, h_ref, gcp_ref, hp_ref, w_ref, o_ref, ext_ref, y_ref):
    t = pl.program_id(1)
    ts = gb_ref.shape[1]
    width = w_ref.shape[0]
    prev = gcp_ref[0].astype(F32) * hp_ref[0].astype(F32)
    ext_ref[0:CONV_HALO, :] = jnp.where(t > 0, prev, 0.0)

    def mul_body(i, carry):
        rows = pl.ds(pl.multiple_of(i * CONV_RC, CONV_RC), CONV_RC)
        ext_ref[pl.ds(pl.multiple_of(CONV_HALO + i * CONV_RC, CONV_HALO), CONV_RC), :] = (
            gc_ref[0, rows, :].astype(F32) * h_ref[0, rows, :].astype(F32))
        return carry

    lax.fori_loop(0, ts // CONV_RC, mul_body, 0)
    _conv_taps(ext_ref, w_ref, y_ref, width,
               lambda cols: jnp.zeros((CONV_RC, cols.stop - cols.start), F32))

    def out_body(i, carry):
        rows = pl.ds(pl.multiple_of(i * CONV_RC, CONV_RC), CONV_RC)
        o_ref[0, rows, :] = (gb_ref[0, rows, :].astype(F32) * y_ref[rows, :]).astype(o_ref.dtype)
        return carry

    lax.fori_loop(0, ts // CONV_RC, out_body, 0)


def _halo_spec(ch, ts, col_block):
    per = ts // CONV_HALO
    return pl.BlockSpec((1, CONV_HALO, ch), lambda b, t: (b, jnp.maximum(t * per - 1, 0), col_block))


def conv_gated(u, val_blk, gate_blk, ch, conv_w, conv_b, norm_g, norm_b):
    B, S, _ = u.shape
    ts = min(CONV_ROWS, S)
    assert conv_w.shape[0] - 1 <= CONV_HALO and S % ts == 0
    cur = lambda blk: pl.BlockSpec((1, ts, ch), lambda b, t: (b, t, blk))
    vec = pl.BlockSpec((1, ch), lambda b, t: (0, 0))
    return pl.pallas_call(
        _conv_gated_kernel,
        grid=(B, S // ts),
        in_specs=[cur(val_blk), cur(gate_blk), _halo_spec(ch, ts, val_blk), _halo_spec(ch, ts, gate_blk),
                  pl.BlockSpec(conv_w.shape, lambda b, t: (0, 0)), vec, vec, vec],
        out_specs=pl.BlockSpec((1, ts, ch), lambda b, t: (b, t, 0)),
        out_shape=jax.ShapeDtypeStruct((B, S, ch), BF16),
        scratch_shapes=[pltpu.VMEM((CONV_HALO + ts, ch), F32), pltpu.VMEM((ts, ch), F32)],
        compiler_params=pltpu.CompilerParams(dimension_semantics=("parallel", "parallel"),
                                             vmem_limit_bytes=VMEM_LIMIT_BYTES),
        name="conv_gated",
    )(u, u, u, u, conv_w, conv_b.reshape(1, ch), norm_g.reshape(1, ch), norm_b.reshape(1, ch))


def conv_short(u, gb_blk, gc_blk, h_blk, ch, conv_w):
    B, S, _ = u.shape
    ts = min(CONV_ROWS, S)
    assert conv_w.shape[0] - 1 <= CONV_HALO and S % ts == 0
    cur = lambda blk: pl.BlockSpec((1, ts, ch), lambda b, t: (b, t, blk))
    return pl.pallas_call(
        _conv_short_kernel,
        grid=(B, S // ts),
        in_specs=[cur(gb_blk), cur(gc_blk), cur(h_blk), _halo_spec(ch, ts, gc_blk), _halo_spec(ch, ts, h_blk),
                  pl.BlockSpec(conv_w.shape, lambda b, t: (0, 0))],
        out_specs=pl.BlockSpec((1, ts, ch), lambda b, t: (b, t, 0)),
        out_shape=jax.ShapeDtypeStruct((B, S, ch), BF16),
        scratch_shapes=[pltpu.VMEM((CONV_HALO + ts, ch), F32), pltpu.VMEM((ts, ch), F32)],
        compiler_params=pltpu.CompilerParams(dimension_semantics=("parallel", "parallel"),
                                             vmem_limit_bytes=VMEM_LIMIT_BYTES),
        name="conv_short",
    )(u, u, u, u, u, conv_w)


def _dilated_kernel(relb_ref, q_ref, kp_ref, kc_ref, vp_ref, vc_ref, o_ref, lse_ref, bias_ref, *, dilation, steps):
    n = pl.program_id(2)
    BLK = DIL_BLOCK
    i_pos = lax.broadcasted_iota(I32, (BLK, 2 * BLK), 0)
    j_pos = lax.broadcasted_iota(I32, (BLK, 2 * BLK), 1)
    m = BLK + i_pos - j_pos

    @pl.when((pl.program_id(0) == 0) & (pl.program_id(1) == 0) & (n == 0))
    def _():
        for h in range(DIL_SLOTS):
            bias_ref[h] = _t5_bias_tile(relb_ref, h, m * dilation)

    valid = (m >= 0) & (m <= steps) & ((n > 0) | (j_pos >= BLK))
    for h in range(DIL_SLOTS):
        cols = slice(h * HEAD_DIM, (h + 1) * HEAD_DIM)
        q = q_ref[0, :, cols]
        kb = jnp.concatenate([kp_ref[0, :, cols], kc_ref[0, :, cols]], axis=0)
        vb = jnp.concatenate([vp_ref[0, :, cols], vc_ref[0, :, cols]], axis=0)
        lg = _nt_dot(q, kb) * (HEAD_DIM ** -0.5) + bias_ref[h]
        lg = jnp.where(valid, lg, NEG_INF)
        mx = jnp.max(lg, axis=-1, keepdims=True)
        p = jnp.exp(lg - mx)
        l = jnp.sum(p, axis=-1, keepdims=True)
        o = jnp.dot(p.astype(BF16), vb, preferred_element_type=F32)
        o_ref[0, :, cols] = o * (1.0 / l)
        lse_ref[0, :, cols] = jnp.broadcast_to(mx + jnp.log(l), (BLK, HEAD_DIM))


def dilated_group(u, rel_bias, q_blk, k_blk, v_blk, window, dilation):
    B, S, N = u.shape
    L = S // dilation
    W = DIL_SLOTS * HEAD_DIM
    assert S % dilation == 0 and L % DIL_BLOCK == 0 and N % W == 0 and window // dilation <= DIL_BLOCK
    per_res = N // W
    uv = u.reshape(B, L, dilation * N)
    cur = lambda blk: pl.BlockSpec((1, DIL_BLOCK, W), lambda b, r, n: (b, n, r * per_res + blk))
    prev = lambda blk: pl.BlockSpec((1, DIL_BLOCK, W), lambda b, r, n: (b, jnp.maximum(n - 1, 0), r * per_res + blk))
    out = pl.BlockSpec((1, DIL_BLOCK, W), lambda b, r, n: (b, n, r))
    kern = functools.partial(_dilated_kernel, dilation=dilation, steps=window // dilation)
    o, lse = pl.pallas_call(
        kern,
        grid=(B, dilation, L // DIL_BLOCK),
        in_specs=[pl.BlockSpec(memory_space=pltpu.SMEM), cur(q_blk), prev(k_blk), cur(k_blk), prev(v_blk), cur(v_blk)],
        out_specs=[out, out],
        out_shape=[jax.ShapeDtypeStruct((B, L, dilation * W), F32)] * 2,
        scratch_shapes=[pltpu.VMEM((DIL_SLOTS, DIL_BLOCK, 2 * DIL_BLOCK), F32)],
        compiler_params=pltpu.CompilerParams(dimension_semantics=("arbitrary", "arbitrary", "arbitrary")),
        name=f"dilated_d{dilation}",
    )(rel_bias, uv, uv, uv, uv, uv)
    return o.reshape(B, S, W), lse.reshape(B, S, W)


def _dil_combine_kernel(o0, o1, o2, l0, l1, l2, out_ref):
    ls = [l0[0], l1[0], l2[0]]
    mx = jnp.maximum(jnp.maximum(ls[0], ls[1]), ls[2])
    es = [jnp.exp(l - mx) for l in ls]
    den = es[0] + es[1] + es[2]
    num = es[0] * o0[0] + es[1] * o1[0] + es[2] * o2[0]
    out_ref[0] = (num / den).astype(out_ref.dtype)


def dilated_combine(outs, lses):
    B, S, W = outs[0].shape
    ts = min(512, S)
    spec = pl.BlockSpec((1, ts, W), lambda b, t: (b, t, 0))
    return pl.pallas_call(
        _dil_combine_kernel,
        grid=(B, S // ts),
        in_specs=[spec] * 6,
        out_specs=spec,
        out_shape=jax.ShapeDtypeStruct((B, S, W), BF16),
        compiler_params=pltpu.CompilerParams(dimension_semantics=("parallel", "parallel")),
        name="dilated_combine",
    )(*outs, *lses)


def dilated_attention(u, rel_bias, q_blk0, k_blk0, v_blk0):
    assert DIL_GROUPS == 3
    outs, lses = [], []
    for g, (window, dilation) in enumerate(DIL_CONFIGS):
        o, lse = dilated_group(u, rel_bias, q_blk0 + g, k_blk0 + g, v_blk0 + g, window, dilation)
        outs.append(o)
        lses.append(lse)
    return dilated_combine(outs, lses)


EVEN_ATTN_COLS = 6144
EVEN_QIDX_COL = 0
EVEN_Q_COL = IDX_HEADS * IDX_DIM
EVEN_CKV_COL = EVEN_Q_COL + DSA_HEADS * HEAD_DIM
EVEN_KIDX_COL = EVEN_CKV_COL + KV_RANK
EVEN_WIDX_COL = EVEN_KIDX_COL + IDX_DIM


def _even_weight_layout(w_in, ca):
    a_val, a_gate, q, c_kv, q_idx, k_idx, w_idx = _split_cols(
        w_in, [ca, ca, DSA_HEADS * HEAD_DIM, KV_RANK, IDX_HEADS * IDX_DIM, IDX_DIM, IDX_HEADS])
    pad = jnp.zeros((w_in.shape[0], EVEN_ATTN_COLS - (EVEN_WIDX_COL + IDX_HEADS)), w_in.dtype)
    return jnp.concatenate([q_idx, q, c_kv, k_idx, w_idx, pad, a_val, a_gate], axis=1).astype(BF16)


def mixer_even(u, conv_w, conv_b, norm_g, norm_b, kv_norm_g, w_uk, w_uv, rel_bias):
    B, S, _ = u.shape
    ca = conv_w.shape[-1]
    assert EVEN_ATTN_COLS % ca == 0
    a = conv_gated(u, EVEN_ATTN_COLS // ca, EVEN_ATTN_COLS // ca + 1, ca, conv_w, conv_b, norm_g, norm_b)
    cn, ct = ckv_norm(u, kv_norm_g.reshape(1, KV_RANK), EVEN_CKV_COL // KV_RANK, min(DSA_TS, S))
    o = dsa_attention(u, cn, ct, w_uk.reshape(KV_RANK, DSA_HEADS * HEAD_DIM).astype(BF16),
                      jnp.transpose(w_uv, (1, 2, 0)).astype(BF16), rel_bias,
                      (EVEN_QIDX_COL // (IDX_HEADS * IDX_DIM), EVEN_Q_COL // (DSA_HEADS * HEAD_DIM),
                       EVEN_WIDX_COL // 128, EVEN_KIDX_COL // IDX_DIM), min(IDX_TOPK, S // 4))
    return jnp.concatenate([a, o], axis=-1)


def mixer_odd(u, conv_w, rel_bias):
    cc = conv_w.shape[-1]
    W = DIL_SLOTS * HEAD_DIM
    assert (3 * cc) % W == 0
    c_out = conv_short(u, 0, 1, 2, cc, conv_w)
    q_blk0 = 3 * cc // W
    o = dilated_attention(u, rel_bias, q_blk0, q_blk0 + DIL_GROUPS, q_blk0 + 2 * DIL_GROUPS)
    return jnp.concatenate([c_out, o], axis=-1)


def kernel(x, w_ffn_in, w_ffn_out, post_ln_g, post_ln_b, rel_bias, w_in_e, w_out_e, conv_a_w, conv_a_b,
           conv_a_norm_g, conv_a_norm_b, kv_norm_g, w_uk, w_uv, w_in_o, w_out_o, conv_c_w):
    B, S, D = x.shape
    M = B * S
    depth = w_ffn_in.shape[0]
    xs = x.reshape(M, D)
    w_ffn_in_b = w_ffn_in.astype(BF16)
    w_ffn_out_b = w_ffn_out.astype(BF16)
    for layer in range(depth):
        g = post_ln_g[layer].reshape(3, 1, D)
        b = post_ln_b[layer].reshape(3, 1, D)
        xs = ffn_ln(xs, w_ffn_in_b[layer, 0], w_ffn_out_b[layer, 0], g[0], b[0])
        if layer % 2 == 0:
            e = layer // 2
            u = project(xs, _even_weight_layout(w_in_e[e], conv_a_w.shape[-1]))
            mix = mixer_even(u.reshape(B, S, -1), conv_a_w[e], conv_a_b[e], conv_a_norm_g[e],
                             conv_a_norm_b[e], kv_norm_g[e], w_uk[e], w_uv[e], rel_bias)
            w_out = w_out_e[e]
        else:
            o = layer // 2
            u = project(xs, w_in_o[o].astype(BF16))
            mix = mixer_odd(u.reshape(B, S, -1), conv_c_w[o], rel_bias)
            w_out = w_out_o[o]
        xs = out_ln(mix.reshape(M, -1), w_out.astype(BF16), xs, g[1], b[1])
        xs = ffn_ln(xs, w_ffn_in_b[layer, 1], w_ffn_out_b[layer, 1], g[2], b[2])
    return xs.reshape(B, S, D)
```

```python
import functools
import math

import jax
import jax.numpy as jnp
import numpy as np
from jax import lax
from jax.experimental import pallas as pl
from jax.experimental.pallas import tpu as pltpu

DEPTH = 2
ALPHA = (2 * DEPTH) ** 0.25
LN_EPS = 1e-5
NEG_INF = -1e30

HEAD_DIM = 128
DSA_HEADS = 8
KV_RANK = 512
IDX_HEADS = 32
IDX_DIM = 128
IDX_TOPK = 256
DIL_SLOTS = 8
DIL_CONFIGS = ((128, 1), (512, 4), (2048, 16))
DIL_GROUPS = len(DIL_CONFIGS)
DIL_HEADS = DIL_SLOTS * DIL_GROUPS
DIL_BLOCK = 128
REL_BUCKETS = 32
REL_MAX_DIST = 128

V7X_VMEM_BYTES = 64 * 1024 * 1024
VMEM_LIMIT_BYTES = V7X_VMEM_BYTES - 6 * 1024 * 1024

BF16 = jnp.bfloat16
F32 = jnp.float32
I32 = jnp.int32
INT_MIN = -2 ** 31
SUBLANES = 8
LANES = 128


def _layer_norm_rows(y, g, b):
    mu = jnp.mean(y, axis=-1, keepdims=True)
    yc = y - mu
    var = jnp.mean(yc * yc, axis=-1, keepdims=True)
    return yc * lax.rsqrt(var + LN_EPS) * g + b


LN_CHUNK = 16
LN_GROUP = 4


def _layer_norm_loop(n_rows, load_fn, store_fn, g, b):
    step = LN_CHUNK * LN_GROUP
    assert n_rows % step == 0

    def body(i, carry):
        base = pl.multiple_of(i * step, step)
        rows = [pl.ds(base + c * LN_CHUNK, LN_CHUNK) for c in range(LN_GROUP)]
        ys = [load_fn(r) for r in rows]
        for r, y in zip(rows, ys):
            store_fn(r, _layer_norm_rows(y, g, b))
        return carry

    lax.fori_loop(0, n_rows // step, body, 0)


def _nt_dot(a, b):
    return lax.dot_general(a, b, (((1,), (1,)), ((), ())), preferred_element_type=F32)


def _split_cols(u, sizes):
    offs = np.cumsum([0] + list(sizes))
    return [u[..., int(offs[i]):int(offs[i + 1])] for i in range(len(sizes))]


FFN_ROWS = 512
FFN_HIDDEN = 256


def _ffn_ln_kernel(x_ref, wg_ref, wu_ref, wo_ref, g_ref, b_ref, o_ref, xb_ref):
    f = pl.program_id(1)

    @pl.when(f == 0)
    def _():
        xb_ref[...] = x_ref[...].astype(BF16)
        o_ref[...] = jnp.zeros_like(o_ref)

    xb = xb_ref[...]
    gate = jnp.dot(xb, wg_ref[...], preferred_element_type=F32)
    up = jnp.dot(xb, wu_ref[...], preferred_element_type=F32)
    h = (gate * jax.nn.sigmoid(gate) * up).astype(BF16)
    o_ref[...] += jnp.dot(h, wo_ref[...], preferred_element_type=F32)

    @pl.when(f == pl.num_programs(1) - 1)
    def _():
        def store(rows, z):
            o_ref[rows, :] = z

        _layer_norm_loop(x_ref.shape[0], lambda rows: ALPHA * x_ref[rows, :] + 0.5 * o_ref[rows, :], store,
                         g_ref[...], b_ref[...])


def ffn_ln(x, w_in, w_out, g, b):
    m, d = x.shape
    f = w_out.shape[0]
    tm = min(FFN_ROWS, m)
    tf = FFN_HIDDEN
    assert m % tm == 0 and f % tf == 0
    nf = f // tf
    return pl.pallas_call(
        _ffn_ln_kernel,
        grid=(m // tm, nf),
        in_specs=[
            pl.BlockSpec((tm, d), lambda i, j: (i, 0)),
            pl.BlockSpec((d, tf), lambda i, j: (0, j)),
            pl.BlockSpec((d, tf), lambda i, j: (0, j + nf)),
            pl.BlockSpec((tf, d), lambda i, j: (j, 0)),
            pl.BlockSpec((1, d), lambda i, j: (0, 0)),
            pl.BlockSpec((1, d), lambda i, j: (0, 0)),
        ],
        out_specs=pl.BlockSpec((tm, d), lambda i, j: (i, 0)),
        out_shape=jax.ShapeDtypeStruct((m, d), F32),
        scratch_shapes=[pltpu.VMEM((tm, d), BF16)],
        compiler_params=pltpu.CompilerParams(
            dimension_semantics=("parallel", "arbitrary"),
            vmem_limit_bytes=VMEM_LIMIT_BYTES),
        name="ffn_ln",
    )(x, w_in, w_in, w_out, g, b)


PROJ_ROWS = 1024
PROJ_COLS = 512


def _proj_kernel(x_ref, w_ref, o_ref, xb_ref, *res_ref, dilation):
    @pl.when(pl.program_id(1) == 0)
    def _():
        xb_ref[...] = x_ref[...].astype(BF16)

    y = jnp.dot(xb_ref[...], w_ref[...], preferred_element_type=F32)
    if dilation == 1:
        o_ref[...] = y.astype(o_ref.dtype)
    else:
        y_ref, = res_ref
        rows = y_ref.shape[1] // dilation
        for c in range(y_ref.shape[0]):
            lanes = slice(c * LANES, (c + 1) * LANES)
            y_ref[c] = y[:, lanes]
            for r in range(dilation):
                o_ref[0, r, :, lanes] = y_ref[c, pl.ds(r, rows, stride=dilation), :].astype(o_ref.dtype)


def project(x, w, seq_len, dilation=1):
    m, k = x.shape
    n = w.shape[1]
    tm = min(PROJ_ROWS, seq_len)
    tn = min(PROJ_COLS, n)
    assert m % seq_len == 0 and seq_len % tm == 0 and n % tn == 0 and tm % (dilation * 16) == 0
    if dilation == 1:
        out_spec = pl.BlockSpec((tm, tn), lambda i, j: (i, j))
        out_shape = jax.ShapeDtypeStruct((m, n), BF16)
        scratch = []
    else:
        per = seq_len // tm
        out_spec = pl.BlockSpec((1, dilation, tm // dilation, tn), lambda i, j: (i // per, 0, i % per, j))
        out_shape = jax.ShapeDtypeStruct((m // seq_len, dilation, seq_len // dilation, n), BF16)
        scratch = [pltpu.VMEM((tn // LANES, tm, LANES), F32)]
    return pl.pallas_call(
        functools.partial(_proj_kernel, dilation=dilation),
        grid=(m // tm, n // tn),
        in_specs=[
            pl.BlockSpec((tm, k), lambda i, j: (i, 0)),
            pl.BlockSpec((k, tn), lambda i, j: (0, j)),
        ],
        out_specs=out_spec,
        out_shape=out_shape,
        scratch_shapes=[pltpu.VMEM((tm, k), BF16)] + scratch,
        compiler_params=pltpu.CompilerParams(
            dimension_semantics=("parallel", "arbitrary"),
            vmem_limit_bytes=VMEM_LIMIT_BYTES),
        name="project",
    )(x, w)


OUT_ROWS = 512
OUT_K = 1024


def _out_ln_kernel(a_ref, w_ref, x_ref, g_ref, b_ref, o_ref):
    kk = pl.program_id(1)

    @pl.when(kk == 0)
    def _():
        o_ref[...] = jnp.zeros_like(o_ref)

    o_ref[...] += jnp.dot(a_ref[...], w_ref[...], preferred_element_type=F32)

    @pl.when(kk == pl.num_programs(1) - 1)
    def _():
        def store(rows, z):
            o_ref[rows, :] = z

        _layer_norm_loop(x_ref.shape[0], lambda rows: ALPHA * x_ref[rows, :] + o_ref[rows, :], store,
                         g_ref[...], b_ref[...])


def out_ln(a, w, x, g, b):
    m, k = a.shape
    d = w.shape[1]
    tm = min(OUT_ROWS, m)
    tk = min(OUT_K, k)
    assert m % tm == 0 and k % tk == 0
    return pl.pallas_call(
        _out_ln_kernel,
        grid=(m // tm, k // tk),
        in_specs=[
            pl.BlockSpec((tm, tk), lambda i, j: (i, j)),
            pl.BlockSpec((tk, d), lambda i, j: (j, 0)),
            pl.BlockSpec((tm, d), lambda i, j: (i, 0)),
            pl.BlockSpec((1, d), lambda i, j: (0, 0)),
            pl.BlockSpec((1, d), lambda i, j: (0, 0)),
        ],
        out_specs=pl.BlockSpec((tm, d), lambda i, j: (i, 0)),
        out_shape=jax.ShapeDtypeStruct((m, d), F32),
        compiler_params=pltpu.CompilerParams(
            dimension_semantics=("parallel", "arbitrary"),
            vmem_limit_bytes=VMEM_LIMIT_BYTES),
        name="out_ln",
    )(a, w, x, g, b)


def _t5_thresholds():
    d = np.arange(0, 2 * REL_MAX_DIST)
    max_exact = REL_BUCKETS // 2
    nf = np.maximum(d, 1).astype(np.float32)
    large = max_exact + (np.log(nf / np.float32(max_exact)) / np.float32(math.log(REL_MAX_DIST / max_exact))
                         * np.float32(REL_BUCKETS - max_exact)).astype(np.int32)
    large = np.minimum(large, REL_BUCKETS - 1)
    bucket = np.where(d < max_exact, d, large)
    return [int(np.argmax(bucket >= k)) for k in range(REL_BUCKETS)]


T5_THR = _t5_thresholds()


def _t5_bias_tile(relb_ref, head, dist):
    acc = jnp.full(dist.shape, relb_ref[0, head], F32)
    for k in range(1, REL_BUCKETS):
        acc = jnp.where(dist >= T5_THR[k], relb_ref[k, head], acc)
    return acc


DSA_TQ = 128
DSA_TS = 512
DSA_HG = 2
SUB = 128
assert T5_THR[REL_BUCKETS - 1] <= SUB


def _ckv_kernel(c_ref, g_ref, cn_ref, ct_ref):
    c = c_ref[0].astype(F32)
    y = c * lax.rsqrt(jnp.mean(c * c, axis=-1, keepdims=True) + LN_EPS) * g_ref[...]
    cn_ref[0] = y.astype(BF16)
    ct_ref[0, 0] = y.T.astype(BF16)


def ckv_norm(u, g, col_block, ts):
    B, S, _ = u.shape
    R = KV_RANK
    return pl.pallas_call(
        _ckv_kernel,
        grid=(B, S // ts),
        in_specs=[pl.BlockSpec((1, ts, R), lambda b, j: (b, j, col_block)),
                  pl.BlockSpec((1, R), lambda b, j: (0, 0))],
        out_specs=[pl.BlockSpec((1, ts, R), lambda b, j: (b, j, 0)),
                   pl.BlockSpec((1, 1, R, ts), lambda b, j: (b, j, 0, 0))],
        out_shape=[jax.ShapeDtypeStruct((B, S, R), BF16),
                   jax.ShapeDtypeStruct((B, S // ts, R, ts), BF16)],
        compiler_params=pltpu.CompilerParams(dimension_semantics=("parallel", "parallel")),
        name="ckv_norm",
    )(u, g)


def _dsa_kernel(relb_ref, qi_ref, q_ref, wi_ref, k_ref, c_ref, ct_ref, wuk_ref, wuvt_ref, o_ref,
                qm_ref, qlt_ref, wt_ref, key_ref, acc_ref, m_ref, l_ref, bias_ref, *, n_keep):
    b = pl.program_id(0)
    qt = pl.program_id(1)
    S = k_ref.shape[1]
    TS = ct_ref.shape[3]
    TQ = DSA_TQ
    H = DSA_HEADS
    n_sub = TS // SUB
    n_s = (qt * TQ + TQ + TS - 1) // TS
    idx_bits = int(S).bit_length()

    @pl.when((b == 0) & (qt == 0))
    def _():
        s_l = lax.broadcasted_iota(I32, (SUB, TQ), 0)
        t_l = lax.broadcasted_iota(I32, (SUB, TQ), 1)
        for which in range(2):
            for h in range(H):
                bias_ref[which, h] = (_t5_bias_tile(relb_ref, h, t_l - s_l + SUB * which)
                                      - relb_ref[REL_BUCKETS - 1, h])

    for h in range(IDX_HEADS):
        qm_ref[h * TQ:(h + 1) * TQ, :] = qi_ref[0, :, h * IDX_DIM:(h + 1) * IDX_DIM]
    for h in range(H):
        ql = _nt_dot(wuk_ref[:, h * HEAD_DIM:(h + 1) * HEAD_DIM], q_ref[0, :, h * HEAD_DIM:(h + 1) * HEAD_DIM])
        qlt_ref[:, h * TQ:(h + 1) * TQ] = (ql * (HEAD_DIM ** -0.5)).astype(BF16)
    wt_ref[...] = wi_ref[0].astype(F32).T * (IDX_HEADS ** -0.5 * IDX_DIM ** -0.5)

    t_pos = qt * TQ + lax.broadcasted_iota(I32, (TS, TQ), 1)
    s_loc = lax.broadcasted_iota(I32, (TS, TQ), 0)

    def idx_body(j, carry):
        row0 = pl.multiple_of(j * TS, TS)
        kt = k_ref[0, pl.ds(row0, TS), :]
        score = jnp.zeros((TS, TQ), F32)
        for hp in range(IDX_HEADS // 2):
            sc = _nt_dot(kt, qm_ref[hp * 2 * TQ:(hp + 1) * 2 * TQ, :])
            for i in range(2):
                h = 2 * hp + i
                score = score + jnp.maximum(sc[:, i * TQ:(i + 1) * TQ], 0.0) * wt_ref[h:h + 1, :]
        bits = pltpu.bitcast(score, I32)
        key = bits ^ ((bits >> 31) & 0x7FFFFFFF)
        key = jnp.where(row0 + s_loc <= t_pos, key, INT_MIN)
        key_ref[pl.ds(row0, TS), :] = key
        return carry

    lax.fori_loop(0, n_s, idx_body, 0)

    def count(pred_fn):
        def body(j, part):
            row0 = pl.multiple_of(j * TS, TS)
            blk = key_ref[pl.ds(row0, TS), :]
            hit = pred_fn(blk, row0).astype(I32)
            return part + hit.reshape(TS // 8, 8, TQ).sum(axis=0)
        part = lax.fori_loop(0, n_s, body, jnp.zeros((8, TQ), I32))
        return part.sum(axis=0, keepdims=True)

    c0 = count(lambda blk, row0: blk >= 0)
    thr = jnp.where(c0 >= n_keep, jnp.zeros((1, TQ), I32), jnp.full((1, TQ), INT_MIN, I32))

    def bit_body(i, thr):
        cand = thr | (jnp.int32(1) << (30 - i))
        c = count(lambda blk, row0: blk >= cand)
        return jnp.where(c >= n_keep, cand, thr)

    thr = lax.fori_loop(0, 31, bit_body, thr)

    c_gt = count(lambda blk, row0: blk > thr)
    c_ge = count(lambda blk, row0: blk >= thr)
    need = n_keep - c_gt

    def tie_search():
        def tie_body(i, x):
            cand = x | (jnp.int32(1) << (idx_bits - 1 - i))
            c = count(lambda blk, row0: (blk == thr) & (row0 + s_loc < cand))
            return jnp.where(c < need, cand, x)
        return lax.fori_loop(0, idx_bits, tie_body, jnp.zeros((1, TQ), I32))

    tie_pos = lax.cond(jnp.max(c_ge) > n_keep, tie_search, lambda: jnp.full((1, TQ), 2 ** idx_bits - 1, I32))

    m_ref[...] = jnp.full(m_ref.shape, NEG_INF, F32)
    l_ref[...] = jnp.zeros(l_ref.shape, F32)
    acc_ref[...] = jnp.zeros(acc_ref.shape, F32)

    def att_tile(j, near):
        row0 = pl.multiple_of(j * TS, TS)
        key = key_ref[pl.ds(row0, TS), :]
        s_pos = row0 + s_loc
        sel = ((key > thr) | ((key == thr) & (s_pos <= tie_pos))) & (s_pos <= t_pos)
        ckv = c_ref[0, pl.ds(row0, TS), :]
        ckv_t = ct_ref[0, j]
        for hg in range(H // DSA_HG):
            gcols = slice(hg * DSA_HG * TQ, (hg + 1) * DSA_HG * TQ)
            lg = jnp.dot(ckv, qlt_ref[:, gcols], preferred_element_type=F32)
            ps, alphas = [], []
            for i in range(DSA_HG):
                h = DSA_HG * hg + i
                cols = slice(h * TQ, (h + 1) * TQ)
                x = lg[:, i * TQ:(i + 1) * TQ]
                if near:
                    pieces = []
                    for sb in range(n_sub):
                        delta = qt - (j * n_sub + sb)
                        pieces.append(jnp.where(delta == 0, bias_ref[0, h],
                                                jnp.where(delta == 1, bias_ref[1, h], 0.0)))
                    x = x + (pieces[0] if n_sub == 1 else jnp.concatenate(pieces, axis=0))
                lh = jnp.where(sel, x, NEG_INF)
                m_old = m_ref[:, cols]
                m_new = jnp.maximum(m_old, jnp.max(lh, axis=0, keepdims=True))
                p = jnp.exp(lh - m_new)
                alpha = jnp.exp(m_old - m_new)
                l_ref[:, cols] = alpha * l_ref[:, cols] + jnp.sum(p, axis=0, keepdims=True)
                m_ref[:, cols] = m_new
                ps.append(p.astype(BF16))
                alphas.append(alpha)
            pv = jnp.dot(ckv_t, jnp.concatenate(ps, axis=1), preferred_element_type=F32)
            acc_ref[:, gcols] = acc_ref[:, gcols] * jnp.concatenate(alphas, axis=1) + pv

    def far_body(j, carry):
        att_tile(j, near=False)
        return carry

    def near_body(j, carry):
        att_tile(j, near=True)
        return carry

    n_far = jnp.maximum(qt - 1, 0) // n_sub
    lax.fori_loop(0, n_far, far_body, 0)
    lax.fori_loop(n_far, n_s, near_body, 0)

    for h in range(H):
        cols = slice(h * TQ, (h + 1) * TQ)
        ol_t = (acc_ref[:, cols] * (1.0 / l_ref[:, cols])).astype(BF16)
        o_t = jnp.dot(wuvt_ref[h], ol_t, preferred_element_type=F32)
        o_ref[0, :, h * HEAD_DIM:(h + 1) * HEAD_DIM] = o_t.T.astype(o_ref.dtype)


def dsa_attention(u, ckv_n, ckv_t, w_uk2, w_uvt, rel_bias, cols, n_keep):
    B, S, _ = u.shape
    ts = ckv_t.shape[3]
    H, R = DSA_HEADS, KV_RANK
    qi_blk, q_blk, wi_blk, k_blk = cols
    kern = functools.partial(_dsa_kernel, n_keep=n_keep)
    return pl.pallas_call(
        kern,
        grid=(B, S // DSA_TQ),
        in_specs=[
            pl.BlockSpec(memory_space=pltpu.SMEM),
            pl.BlockSpec((1, DSA_TQ, IDX_HEADS * IDX_DIM), lambda b, t: (b, t, qi_blk)),
            pl.BlockSpec((1, DSA_TQ, H * HEAD_DIM), lambda b, t: (b, t, q_blk)),
            pl.BlockSpec((1, DSA_TQ, 128), lambda b, t: (b, t, wi_blk)),
            pl.BlockSpec((1, S, IDX_DIM), lambda b, t: (b, 0, k_blk)),
            pl.BlockSpec((1, S, R), lambda b, t: (b, 0, 0)),
            pl.BlockSpec((1, S // ts, R, ts), lambda b, t: (b, 0, 0, 0)),
            pl.BlockSpec((R, H * HEAD_DIM), lambda b, t: (0, 0)),
            pl.BlockSpec((H, HEAD_DIM, R), lambda b, t: (0, 0, 0)),
        ],
        out_specs=pl.BlockSpec((1, DSA_TQ, H * HEAD_DIM), lambda b, t: (b, t, 0)),
        out_shape=jax.ShapeDtypeStruct((B, S, H * HEAD_DIM), BF16),
        scratch_shapes=[
            pltpu.VMEM((IDX_HEADS * DSA_TQ, IDX_DIM), BF16),
            pltpu.VMEM((R, H * DSA_TQ), BF16),
            pltpu.VMEM((128, DSA_TQ), F32),
            pltpu.VMEM((S, DSA_TQ), I32),
            pltpu.VMEM((R, H * DSA_TQ), F32),
            pltpu.VMEM((1, H * DSA_TQ), F32),
            pltpu.VMEM((1, H * DSA_TQ), F32),
            pltpu.VMEM((2, H, SUB, DSA_TQ), F32),
        ],
        compiler_params=pltpu.CompilerParams(
            dimension_semantics=("arbitrary", "arbitrary"),
            vmem_limit_bytes=VMEM_LIMIT_BYTES),
        name="dsa_attention",
    )(rel_bias, u, u, u, u, ckv_n, ckv_t, w_uk2, w_uvt)


CONV_ROWS = 256
CONV_HALO = 32
CONV_RC = 64
CONV_CW = 512


def _conv_taps(ext_ref, w_ref, y_ref, width, init_fn):
    ts, ch = y_ref.shape
    cw = min(CONV_CW, ch)
    n = CONV_RC + SUBLANES
    assert CONV_HALO >= SUBLANES * ((width - 1) // SUBLANES + 1)

    def body(i, carry):
        r0 = pl.multiple_of(i * CONV_RC, CONV_RC)
        for cc in range(ch // cw):
            cols = slice(cc * cw, (cc + 1) * cw)
            acc = init_fn(cols)
            for b in range(min(SUBLANES, width)):
                z = None
                for a in range((width - b + SUBLANES - 1) // SUBLANES):
                    j = SUBLANES * a + b
                    win = ext_ref[pl.ds(r0 + (CONV_HALO - SUBLANES * (a + 1)), n), cols]
                    term = win * w_ref[width - 1 - j:width - j, cols]
                    z = term if z is None else z + term
                if b == 0:
                    acc = acc + z[SUBLANES:, :]
                else:
                    acc = acc + pltpu.roll(z, n - (SUBLANES - b), axis=0)[:CONV_RC, :]
            y_ref[pl.ds(r0, CONV_RC), cols] = acc
        return carry

    lax.fori_loop(0, ts // CONV_RC, body, 0)


def _conv_gated_kernel(v_ref, g_ref, vp_ref, gp_ref, w_ref, cb_ref, ng_ref, nb_ref, o_ref, ext_ref, y_ref):
    t = pl.program_id(1)
    ts = v_ref.shape[1]
    width = w_ref.shape[0]
    prev = vp_ref[0].astype(F32) * jax.nn.sigmoid(gp_ref[0].astype(F32))
    ext_ref[0:CONV_HALO, :] = jnp.where(t > 0, prev, 0.0)

    def glu_body(i, carry):
        rows = pl.ds(pl.multiple_of(i * CONV_RC, CONV_RC), CONV_RC)
        ext_ref[pl.ds(pl.multiple_of(CONV_HALO + i * CONV_RC, CONV_HALO), CONV_RC), :] = (
            v_ref[0, rows, :].astype(F32) * jax.nn.sigmoid(g_ref[0, rows, :].astype(F32)))
        return carry

    lax.fori_loop(0, ts // CONV_RC, glu_body, 0)
    _conv_taps(ext_ref, w_ref, y_ref, width,
               lambda cols: jnp.broadcast_to(cb_ref[:, cols], (CONV_RC, cols.stop - cols.start)))
    def store(rows, z):
        o_ref[0, rows, :] = (z * jax.nn.sigmoid(z)).astype(o_ref.dtype)

    _layer_norm_loop(ts, lambda rows: y_ref[rows, :], store, ng_ref[...], nb_ref[...])


def _conv_short_kernel(gb_ref, gc_ref, h_ref, gcp_ref, hp_ref, w_ref, o_ref, ext_ref, y_ref):
    t = pl.program_id(1)
    ts = gb_ref.shape[1]
    width = w_ref.shape[0]
    prev = gcp_ref[0].astype(F32) * hp_ref[0].astype(F32)
    ext_ref[0:CONV_HALO, :] = jnp.where(t > 0, prev, 0.0)

    def mul_body(i, carry):
        rows = pl.ds(pl.multiple_of(i * CONV_RC, CONV_RC), CONV_RC)
        ext_ref[pl.ds(pl.multiple_of(CONV_HALO + i * CONV_RC, CONV_HALO), CONV_RC), :] = (
            gc_ref[0, rows, :].astype(F32) * h_ref[0, rows, :].astype(F32))
        return carry

    lax.fori_loop(0, ts // CONV_RC, mul_body, 0)
    _conv_taps(ext_ref, w_ref, y_ref, width,
               lambda cols: jnp.zeros((CONV_RC, cols.stop - cols.start), F32))

    def out_body(i, carry):
        rows = pl.ds(pl.multiple_of(i * CONV_RC, CONV_RC), CONV_RC)
        o_ref[0, rows, :] = (gb_ref[0, rows, :].astype(F32) * y_ref[rows, :]).astype(o_ref.dtype)
        return carry

    lax.fori_loop(0, ts // CONV_RC, out_body, 0)


def _halo_spec(ch, ts, col_block):
    per = ts // CONV_HALO
    return pl.BlockSpec((1, CONV_HALO, ch), lambda b, t: (b, jnp.maximum(t * per - 1, 0), col_block))


def conv_gated(u, val_blk, gate_blk, ch, conv_w, conv_b, norm_g, norm_b):
    B, S, _ = u.shape
    ts = min(CONV_ROWS, S)
    assert conv_w.shape[0] - 1 <= CONV_HALO and S % ts == 0
    cur = lambda blk: pl.BlockSpec((1, ts, ch), lambda b, t: (b, t, blk))
    vec = pl.BlockSpec((1, ch), lambda b, t: (0, 0))
    return pl.pallas_call(
        _conv_gated_kernel,
        grid=(B, S // ts),
        in_specs=[cur(val_blk), cur(gate_blk), _halo_spec(ch, ts, val_blk), _halo_spec(ch, ts, gate_blk),
                  pl.BlockSpec(conv_w.shape, lambda b, t: (0, 0)), vec, vec, vec],
        out_specs=pl.BlockSpec((1, ts, ch), lambda b, t: (b, t, 0)),
        out_shape=jax.ShapeDtypeStruct((B, S, ch), BF16),
        scratch_shapes=[pltpu.VMEM((CONV_HALO + ts, ch), F32), pltpu.VMEM((ts, ch), F32)],
        compiler_params=pltpu.CompilerParams(dimension_semantics=("parallel", "parallel"),
                                             vmem_limit_bytes=VMEM_LIMIT_BYTES),
        name="conv_gated",
    )(u, u, u, u, conv_w, conv_b.reshape(1, ch), norm_g.reshape(1, ch), norm_b.reshape(1, ch))


def conv_short(u, gb_blk, gc_blk, h_blk, ch, conv_w):
    B, S, _ = u.shape
    ts = min(CONV_ROWS, S)
    assert conv_w.shape[0] - 1 <= CONV_HALO and S % ts == 0
    cur = lambda blk: pl.BlockSpec((1, ts, ch), lambda b, t: (b, t, blk))
    return pl.pallas_call(
        _conv_short_kernel,
        grid=(B, S // ts),
        in_specs=[cur(gb_blk), cur(gc_blk), cur(h_blk), _halo_spec(ch, ts, gc_blk), _halo_spec(ch, ts, h_blk),
                  pl.BlockSpec(conv_w.shape, lambda b, t: (0, 0))],
        out_specs=pl.BlockSpec((1, ts, ch), lambda b, t: (b, t, 0)),
        out_shape=jax.ShapeDtypeStruct((B, S, ch), BF16),
        scratch_shapes=[pltpu.VMEM((CONV_HALO + ts, ch), F32), pltpu.VMEM((ts, ch), F32)],
        compiler_params=pltpu.CompilerParams(dimension_semantics=("parallel", "parallel"),
                                             vmem_limit_bytes=VMEM_LIMIT_BYTES),
        name="conv_short",
    )(u, u, u, u, u, conv_w)


def _dilated_kernel(relb_ref, q_ref, kp_ref, kc_ref, vp_ref, vc_ref, o_ref, lse_ref, bias_ref, *, dilation, steps):
    n = pl.program_id(2)
    BLK = DIL_BLOCK
    i_pos = lax.broadcasted_iota(I32, (BLK, 2 * BLK), 0)
    j_pos = lax.broadcasted_iota(I32, (BLK, 2 * BLK), 1)
    m = BLK + i_pos - j_pos

    @pl.when((pl.program_id(0) == 0) & (pl.program_id(1) == 0) & (n == 0))
    def _():
        for h in range(DIL_SLOTS):
            bias_ref[h] = _t5_bias_tile(relb_ref, h, m * dilation)

    valid = (m >= 0) & (m <= steps) & ((n > 0) | (j_pos >= BLK))
    for h in range(DIL_SLOTS):
        cols = slice(h * HEAD_DIM, (h + 1) * HEAD_DIM)
        q = q_ref[0, 0, :, cols]
        kb = jnp.concatenate([kp_ref[0, 0, :, cols], kc_ref[0, 0, :, cols]], axis=0)
        vb = jnp.concatenate([vp_ref[0, 0, :, cols], vc_ref[0, 0, :, cols]], axis=0)
        lg = _nt_dot(q, kb) * (HEAD_DIM ** -0.5) + bias_ref[h]
        lg = jnp.where(valid, lg, NEG_INF)
        mx = jnp.max(lg, axis=-1, keepdims=True)
        p = jnp.exp(lg - mx)
        l = jnp.sum(p, axis=-1, keepdims=True)
        o = jnp.dot(p.astype(BF16), vb, preferred_element_type=F32)
        o_ref[0, 0, :, cols] = o * (1.0 / l)
        lse_ref[0, 0, :, cols] = jnp.broadcast_to(mx + jnp.log(l), (BLK, HEAD_DIM))


def dilated_group(qkv, rel_bias, q_blk, k_blk, v_blk, window):
    B, dilation, L, N = qkv.shape
    W = DIL_SLOTS * HEAD_DIM
    assert L % DIL_BLOCK == 0 and N % W == 0 and window // dilation <= DIL_BLOCK
    cur = lambda blk: pl.BlockSpec((1, 1, DIL_BLOCK, W), lambda b, r, n: (b, r, n, blk))
    prev = lambda blk: pl.BlockSpec((1, 1, DIL_BLOCK, W), lambda b, r, n: (b, r, jnp.maximum(n - 1, 0), blk))
    out = pl.BlockSpec((1, 1, DIL_BLOCK, W), lambda b, r, n: (b, r, n, 0))
    kern = functools.partial(_dilated_kernel, dilation=dilation, steps=window // dilation)
    return pl.pallas_call(
        kern,
        grid=(B, dilation, L // DIL_BLOCK),
        in_specs=[pl.BlockSpec(memory_space=pltpu.SMEM), cur(q_blk), prev(k_blk), cur(k_blk), prev(v_blk), cur(v_blk)],
        out_specs=[out, out],
        out_shape=[jax.ShapeDtypeStruct((B, dilation, L, W), F32)] * 2,
        scratch_shapes=[pltpu.VMEM((DIL_SLOTS, DIL_BLOCK, 2 * DIL_BLOCK), F32)],
        compiler_params=pltpu.CompilerParams(dimension_semantics=("arbitrary", "arbitrary", "arbitrary")),
        name=f"dilated_d{dilation}",
    )(rel_bias, qkv, qkv, qkv, qkv, qkv)


def _dil_combine_kernel(*refs, dilations):
    g = len(dilations)
    o_refs, l_refs, out_ref, tmp_refs = refs[:g], refs[g:2 * g], refs[2 * g], refs[2 * g + 1:]

    def in_position_order(ref, tmp_ref, d):
        if d == 1:
            return ref[0, 0]
        rows = ref.shape[2]
        for c in range(tmp_ref.shape[0]):
            for r in range(d):
                tmp_ref[c, pl.ds(r, rows, stride=d), :] = ref[0, r, :, c * LANES:(c + 1) * LANES]
        return jnp.concatenate([tmp_ref[c] for c in range(tmp_ref.shape[0])], axis=1)

    ls = [in_position_order(l_refs[i], tmp_refs[2 * i], d) for i, d in enumerate(dilations)]
    mx = functools.reduce(jnp.maximum, ls)
    es = [jnp.exp(l - mx) for l in ls]
    den = functools.reduce(jnp.add, es)
    num = None
    for i, d in enumerate(dilations):
        term = es[i] * in_position_order(o_refs[i], tmp_refs[2 * i + 1], d)
        num = term if num is None else num + term
    out_ref[0] = (num / den).astype(out_ref.dtype)


DIL_COMBINE_ROWS = 512


def dilated_combine(outs, lses):
    B, _, _, W = outs[0].shape
    dilations = tuple(o.shape[1] for o in outs)
    S = outs[0].shape[1] * outs[0].shape[2]
    ts = min(DIL_COMBINE_ROWS, S)
    assert all(ts % (d * SUBLANES) == 0 for d in dilations)
    specs = [pl.BlockSpec((1, d, ts // d, W), lambda b, t: (b, 0, t, 0)) for d in dilations]
    return pl.pallas_call(
        functools.partial(_dil_combine_kernel, dilations=dilations),
        grid=(B, S // ts),
        in_specs=specs + specs,
        out_specs=pl.BlockSpec((1, ts, W), lambda b, t: (b, t, 0)),
        out_shape=jax.ShapeDtypeStruct((B, S, W), BF16),
        scratch_shapes=[pltpu.VMEM((W // LANES, ts, LANES), F32)] * (2 * len(dilations)),
        compiler_params=pltpu.CompilerParams(dimension_semantics=("parallel", "parallel"),
                                             vmem_limit_bytes=VMEM_LIMIT_BYTES),
        name="dilated_combine",
    )(*outs, *lses)


EVEN_ATTN_COLS = 6144
EVEN_QIDX_COL = 0
EVEN_Q_COL = IDX_HEADS * IDX_DIM
EVEN_CKV_COL = EVEN_Q_COL + DSA_HEADS * HEAD_DIM
EVEN_KIDX_COL = EVEN_CKV_COL + KV_RANK
EVEN_WIDX_COL = EVEN_KIDX_COL + IDX_DIM


def _even_weight_layout(w_in, ca):
    a_val, a_gate, q, c_kv, q_idx, k_idx, w_idx = _split_cols(
        w_in, [ca, ca, DSA_HEADS * HEAD_DIM, KV_RANK, IDX_HEADS * IDX_DIM, IDX_DIM, IDX_HEADS])
    pad = jnp.zeros((w_in.shape[0], EVEN_ATTN_COLS - (EVEN_WIDX_COL + IDX_HEADS)), w_in.dtype)
    return jnp.concatenate([q_idx, q, c_kv, k_idx, w_idx, pad, a_val, a_gate], axis=1).astype(BF16)


def mixer_even(u, conv_w, conv_b, norm_g, norm_b, kv_norm_g, w_uk, w_uv, rel_bias):
    B, S, _ = u.shape
    ca = conv_w.shape[-1]
    assert EVEN_ATTN_COLS % ca == 0
    a = conv_gated(u, EVEN_ATTN_COLS // ca, EVEN_ATTN_COLS // ca + 1, ca, conv_w, conv_b, norm_g, norm_b)
    cn, ct = ckv_norm(u, kv_norm_g.reshape(1, KV_RANK), EVEN_CKV_COL // KV_RANK, min(DSA_TS, S))
    o = dsa_attention(u, cn, ct, w_uk.reshape(KV_RANK, DSA_HEADS * HEAD_DIM).astype(BF16),
                      jnp.transpose(w_uv, (1, 2, 0)).astype(BF16), rel_bias,
                      (EVEN_QIDX_COL // (IDX_HEADS * IDX_DIM), EVEN_Q_COL // (DSA_HEADS * HEAD_DIM),
                       EVEN_WIDX_COL // 128, EVEN_KIDX_COL // IDX_DIM), min(IDX_TOPK, S // 4))
    return jnp.concatenate([a, o], axis=-1)


def _odd_weight_layout(w_in, cc):
    W = DIL_SLOTS * HEAD_DIM
    conv, qkv = w_in[:, :3 * cc], w_in[:, 3 * cc:]
    group = lambda g: [qkv[:, (i * DIL_GROUPS + g) * W:(i * DIL_GROUPS + g + 1) * W] for i in range(3)]
    main = jnp.concatenate([conv] + group(0), axis=1).astype(BF16)
    return main, [jnp.concatenate(group(g), axis=1).astype(BF16) for g in range(1, DIL_GROUPS)]


def mixer_odd(xs, w_in, conv_w, rel_bias, B, S):
    cc = conv_w.shape[-1]
    W = DIL_SLOTS * HEAD_DIM
    assert (3 * cc) % W == 0 and DIL_CONFIGS[0][1] == 1
    w_main, w_groups = _odd_weight_layout(w_in, cc)
    u = project(xs, w_main, S).reshape(B, S, -1)
    c_out = conv_short(u, 0, 1, 2, cc, conv_w)
    q_blk = 3 * cc // W
    results = [dilated_group(u.reshape(B, 1, S, -1), rel_bias, q_blk, q_blk + 1, q_blk + 2, DIL_CONFIGS[0][0])]
    for w_g, (window, dilation) in zip(w_groups, DIL_CONFIGS[1:]):
        results.append(dilated_group(project(xs, w_g, S, dilation), rel_bias, 0, 1, 2, window))
    o = dilated_combine([r[0] for r in results], [r[1] for r in results])
    return jnp.concatenate([c_out, o], axis=-1)


def kernel(x, w_ffn_in, w_ffn_out, post_ln_g, post_ln_b, rel_bias, w_in_e, w_out_e, conv_a_w, conv_a_b,
           conv_a_norm_g, conv_a_norm_b, kv_norm_g, w_uk, w_uv, w_in_o, w_out_o, conv_c_w):
    B, S, D = x.shape
    M = B * S
    depth = w_ffn_in.shape[0]
    xs = x.reshape(M, D)
    w_ffn_in_b = w_ffn_in.astype(BF16)
    w_ffn_out_b = w_ffn_out.astype(BF16)
    for layer in range(depth):
        g = post_ln_g[layer].reshape(3, 1, D)
        b = post_ln_b[layer].reshape(3, 1, D)
        xs = ffn_ln(xs, w_ffn_in_b[layer, 0], w_ffn_out_b[layer, 0], g[0], b[0])
        if layer % 2 == 0:
            e = layer // 2
            u = project(xs, _even_weight_layout(w_in_e[e], conv_a_w.shape[-1]), S)
            mix = mixer_even(u.reshape(B, S, -1), conv_a_w[e], conv_a_b[e], conv_a_norm_g[e],
                             conv_a_norm_b[e], kv_norm_g[e], w_uk[e], w_uv[e], rel_bias)
            w_out = w_out_e[e]
        else:
            o = layer // 2
            mix = mixer_odd(xs, w_in_o[o], conv_c_w[o], rel_bias, B, S)
            w_out = w_out_o[o]
        xs = out_ln(mix.reshape(M, -1), w_out.astype(BF16), xs, g[1], b[1])
        xs = ffn_ln(xs, w_ffn_in_b[layer, 1], w_ffn_out_b[layer, 1], g[2], b[2])
    return xs.reshape(B, S, D)
```

```python
import functools
import math

import jax
import jax.numpy as jnp
import numpy as np
from jax import lax
from jax.experimental import pallas as pl
from jax.experimental.pallas import tpu as pltpu

DEPTH = 2
ALPHA = (2 * DEPTH) ** 0.25
LN_EPS = 1e-5
NEG_INF = -1e30

HEAD_DIM = 128
DSA_HEADS = 8
KV_RANK = 512
IDX_HEADS = 32
IDX_DIM = 128
IDX_TOPK = 256
DIL_SLOTS = 8
DIL_CONFIGS = ((128, 1), (512, 4), (2048, 16))
DIL_GROUPS = len(DIL_CONFIGS)
DIL_HEADS = DIL_SLOTS * DIL_GROUPS
DIL_BLOCK = 128
REL_BUCKETS = 32
REL_MAX_DIST = 128

V7X_VMEM_BYTES = 64 * 1024 * 1024
VMEM_LIMIT_BYTES = V7X_VMEM_BYTES - 6 * 1024 * 1024

BF16 = jnp.bfloat16
F32 = jnp.float32
I32 = jnp.int32
INT_MIN = -2 ** 31
SUBLANES = 8
LANES = 128


def _layer_norm_rows(y, g, b):
    mu = jnp.mean(y, axis=-1, keepdims=True)
    yc = y - mu
    var = jnp.mean(yc * yc, axis=-1, keepdims=True)
    return yc * lax.rsqrt(var + LN_EPS) * g + b


LN_CHUNK = 16
LN_GROUP = 4


def _layer_norm_loop(n_rows, load_fn, store_fn, g, b):
    step = LN_CHUNK * LN_GROUP
    assert n_rows % step == 0

    def body(i, carry):
        base = pl.multiple_of(i * step, step)
        rows = [pl.ds(base + c * LN_CHUNK, LN_CHUNK) for c in range(LN_GROUP)]
        ys = [load_fn(r) for r in rows]
        for r, y in zip(rows, ys):
            store_fn(r, _layer_norm_rows(y, g, b))
        return carry

    lax.fori_loop(0, n_rows // step, body, 0)


def _nt_dot(a, b):
    return lax.dot_general(a, b, (((1,), (1,)), ((), ())), preferred_element_type=F32)


FFN_ROWS = 512
FFN_HIDDEN = 256


def _ffn_ln_kernel(x_ref, wg_ref, wu_ref, wo_ref, g_ref, b_ref, o_ref, xb_ref):
    f = pl.program_id(1)

    @pl.when(f == 0)
    def _():
        xb_ref[...] = x_ref[...].astype(BF16)
        o_ref[...] = jnp.zeros_like(o_ref)

    xb = xb_ref[...]
    gate = jnp.dot(xb, wg_ref[...], preferred_element_type=F32)
    up = jnp.dot(xb, wu_ref[...], preferred_element_type=F32)
    h = (gate * jax.nn.sigmoid(gate) * up).astype(BF16)
    o_ref[...] += jnp.dot(h, wo_ref[...], preferred_element_type=F32)

    @pl.when(f == pl.num_programs(1) - 1)
    def _():
        def store(rows, z):
            o_ref[rows, :] = z

        _layer_norm_loop(x_ref.shape[0], lambda rows: ALPHA * x_ref[rows, :] + 0.5 * o_ref[rows, :], store,
                         g_ref[...], b_ref[...])


def ffn_ln(x, w_in, w_out, g, b):
    m, d = x.shape
    f = w_out.shape[0]
    tm = min(FFN_ROWS, m)
    tf = FFN_HIDDEN
    assert m % tm == 0 and f % tf == 0
    nf = f // tf
    return pl.pallas_call(
        _ffn_ln_kernel,
        grid=(m // tm, nf),
        in_specs=[
            pl.BlockSpec((tm, d), lambda i, j: (i, 0)),
            pl.BlockSpec((d, tf), lambda i, j: (0, j)),
            pl.BlockSpec((d, tf), lambda i, j: (0, j + nf)),
            pl.BlockSpec((tf, d), lambda i, j: (j, 0)),
            pl.BlockSpec((1, d), lambda i, j: (0, 0)),
            pl.BlockSpec((1, d), lambda i, j: (0, 0)),
        ],
        out_specs=pl.BlockSpec((tm, d), lambda i, j: (i, 0)),
        out_shape=jax.ShapeDtypeStruct((m, d), F32),
        scratch_shapes=[pltpu.VMEM((tm, d), BF16)],
        compiler_params=pltpu.CompilerParams(
            dimension_semantics=("parallel", "arbitrary"),
            vmem_limit_bytes=VMEM_LIMIT_BYTES),
        name="ffn_ln",
    )(x, w_in, w_in, w_out, g, b)


PROJ_ROWS = 1024
PROJ_COLS = 512


def _proj_kernel(tiles_ref, x_ref, w_ref, o_ref, xb_ref, *res_ref, dilation):
    del tiles_ref

    @pl.when(pl.program_id(1) == 0)
    def _():
        xb_ref[...] = x_ref[...].astype(BF16)

    y = jnp.dot(xb_ref[...], w_ref[...], preferred_element_type=F32)
    if dilation == 1:
        o_ref[...] = y.astype(o_ref.dtype)
    else:
        y_ref, = res_ref
        rows = y_ref.shape[1] // dilation
        for c in range(y_ref.shape[0]):
            lanes = slice(c * LANES, (c + 1) * LANES)
            y_ref[c] = y[:, lanes]
            for r in range(dilation):
                o_ref[0, r, :, lanes] = y_ref[c, pl.ds(r, rows, stride=dilation), :].astype(o_ref.dtype)


def _col_tiles(*pieces):
    tiles = []
    for start, width in pieces:
        assert start % PROJ_COLS == 0 and width % PROJ_COLS == 0
        tiles += list(range(start // PROJ_COLS, (start + width) // PROJ_COLS))
    return np.asarray(tiles, np.int32)


def project(x, w, col_tiles, seq_len, dilation=1):
    m, k = x.shape
    tm = min(PROJ_ROWS, seq_len)
    tn = PROJ_COLS
    n = tn * len(col_tiles)
    assert m % seq_len == 0 and seq_len % tm == 0 and w.shape[1] % tn == 0 and tm % (dilation * 16) == 0
    if dilation == 1:
        out_spec = pl.BlockSpec((tm, tn), lambda i, j, tiles: (i, j))
        out_shape = jax.ShapeDtypeStruct((m, n), BF16)
        scratch = []
    else:
        per = seq_len // tm
        out_spec = pl.BlockSpec((1, dilation, tm // dilation, tn), lambda i, j, tiles: (i // per, 0, i % per, j))
        out_shape = jax.ShapeDtypeStruct((m // seq_len, dilation, seq_len // dilation, n), BF16)
        scratch = [pltpu.VMEM((tn // LANES, tm, LANES), F32)]
    return pl.pallas_call(
        functools.partial(_proj_kernel, dilation=dilation),
        grid_spec=pltpu.PrefetchScalarGridSpec(
            num_scalar_prefetch=1,
            grid=(m // tm, len(col_tiles)),
            in_specs=[
                pl.BlockSpec((tm, k), lambda i, j, tiles: (i, 0)),
                pl.BlockSpec((k, tn), lambda i, j, tiles: (0, tiles[j])),
            ],
            out_specs=out_spec,
            scratch_shapes=[pltpu.VMEM((tm, k), BF16)] + scratch,
        ),
        out_shape=out_shape,
        compiler_params=pltpu.CompilerParams(
            dimension_semantics=("parallel", "arbitrary"),
            vmem_limit_bytes=VMEM_LIMIT_BYTES),
        name="project",
    )(jnp.asarray(col_tiles), x, w)


OUT_ROWS = 512
OUT_K = 1024


def _out_ln_kernel(a_ref, a2_ref, w_ref, x_ref, g_ref, b_ref, o_ref, *, n_first):
    kk = pl.program_id(1)

    @pl.when(kk == 0)
    def _():
        o_ref[...] = jnp.zeros_like(o_ref)

    @pl.when(kk < n_first)
    def _():
        o_ref[...] += jnp.dot(a_ref[...], w_ref[...], preferred_element_type=F32)

    @pl.when(kk >= n_first)
    def _():
        o_ref[...] += jnp.dot(a2_ref[...], w_ref[...], preferred_element_type=F32)

    @pl.when(kk == pl.num_programs(1) - 1)
    def _():
        def store(rows, z):
            o_ref[rows, :] = z

        _layer_norm_loop(x_ref.shape[0], lambda rows: ALPHA * x_ref[rows, :] + o_ref[rows, :], store,
                         g_ref[...], b_ref[...])


def out_ln(a, a2, w, x, g, b):
    m, k1 = a.shape
    k2 = a2.shape[1]
    d = w.shape[1]
    tm = min(OUT_ROWS, m)
    tk = min(OUT_K, k1, k2)
    assert m % tm == 0 and k1 % tk == 0 and k2 % tk == 0 and w.shape[0] == k1 + k2
    n1, n2 = k1 // tk, k2 // tk
    return pl.pallas_call(
        functools.partial(_out_ln_kernel, n_first=n1),
        grid=(m // tm, n1 + n2),
        in_specs=[
            pl.BlockSpec((tm, tk), lambda i, j: (i, jnp.minimum(j, n1 - 1))),
            pl.BlockSpec((tm, tk), lambda i, j: (i, jnp.maximum(j - n1, 0))),
            pl.BlockSpec((tk, d), lambda i, j: (j, 0)),
            pl.BlockSpec((tm, d), lambda i, j: (i, 0)),
            pl.BlockSpec((1, d), lambda i, j: (0, 0)),
            pl.BlockSpec((1, d), lambda i, j: (0, 0)),
        ],
        out_specs=pl.BlockSpec((tm, d), lambda i, j: (i, 0)),
        out_shape=jax.ShapeDtypeStruct((m, d), F32),
        compiler_params=pltpu.CompilerParams(
            dimension_semantics=("parallel", "arbitrary"),
            vmem_limit_bytes=VMEM_LIMIT_BYTES),
        name="out_ln",
    )(a, a2, w, x, g, b)


def _t5_thresholds():
    d = np.arange(0, 2 * REL_MAX_DIST)
    max_exact = REL_BUCKETS // 2
    nf = np.maximum(d, 1).astype(np.float32)
    large = max_exact + (np.log(nf / np.float32(max_exact)) / np.float32(math.log(REL_MAX_DIST / max_exact))
                         * np.float32(REL_BUCKETS - max_exact)).astype(np.int32)
    large = np.minimum(large, REL_BUCKETS - 1)
    bucket = np.where(d < max_exact, d, large)
    return [int(np.argmax(bucket >= k)) for k in range(REL_BUCKETS)]


T5_THR = _t5_thresholds()


def _t5_bias_tile(relb_ref, head, dist):
    acc = jnp.full(dist.shape, relb_ref[0, head], F32)
    for k in range(1, REL_BUCKETS):
        acc = jnp.where(dist >= T5_THR[k], relb_ref[k, head], acc)
    return acc


DSA_TQ = 128
DSA_TS = 512
DSA_HG = 2
SUB = 128
assert T5_THR[REL_BUCKETS - 1] <= SUB


def _ckv_kernel(c_ref, g_ref, cn_ref, ct_ref):
    c = c_ref[0].astype(F32)
    y = c * lax.rsqrt(jnp.mean(c * c, axis=-1, keepdims=True) + LN_EPS) * g_ref[...]
    cn_ref[0] = y.astype(BF16)
    ct_ref[0, 0] = y.T.astype(BF16)


def ckv_norm(u, g, col_block, ts):
    B, S, _ = u.shape
    R = KV_RANK
    return pl.pallas_call(
        _ckv_kernel,
        grid=(B, S // ts),
        in_specs=[pl.BlockSpec((1, ts, R), lambda b, j: (b, j, col_block)),
                  pl.BlockSpec((1, R), lambda b, j: (0, 0))],
        out_specs=[pl.BlockSpec((1, ts, R), lambda b, j: (b, j, 0)),
                   pl.BlockSpec((1, 1, R, ts), lambda b, j: (b, j, 0, 0))],
        out_shape=[jax.ShapeDtypeStruct((B, S, R), BF16),
                   jax.ShapeDtypeStruct((B, S // ts, R, ts), BF16)],
        compiler_params=pltpu.CompilerParams(dimension_semantics=("parallel", "parallel")),
        name="ckv_norm",
    )(u, g)


def _dsa_kernel(relb_ref, qi_ref, q_ref, wi_ref, k_ref, c_ref, ct_ref, wuk_ref, wuvt_ref, o_ref,
                qm_ref, qlt_ref, wt_ref, key_ref, acc_ref, m_ref, l_ref, bias_ref, p_ref, a_ref, *, n_keep):
    b = pl.program_id(0)
    qt = pl.program_id(1)
    S = k_ref.shape[1]
    TS = ct_ref.shape[3]
    TQ = DSA_TQ
    H = DSA_HEADS
    n_sub = TS // SUB
    n_s = (qt * TQ + TQ + TS - 1) // TS
    idx_bits = int(S).bit_length()

    @pl.when((b == 0) & (qt == 0))
    def _():
        s_l = lax.broadcasted_iota(I32, (SUB, TQ), 0)
        t_l = lax.broadcasted_iota(I32, (SUB, TQ), 1)
        for which in range(2):
            for h in range(H):
                bias_ref[which, h] = (_t5_bias_tile(relb_ref, h, t_l - s_l + SUB * which)
                                      - relb_ref[REL_BUCKETS - 1, h])

    for h in range(IDX_HEADS):
        qm_ref[h * TQ:(h + 1) * TQ, :] = qi_ref[0, :, h * IDX_DIM:(h + 1) * IDX_DIM]
    for h in range(H):
        ql = _nt_dot(wuk_ref[:, h * HEAD_DIM:(h + 1) * HEAD_DIM], q_ref[0, :, h * HEAD_DIM:(h + 1) * HEAD_DIM])
        qlt_ref[:, h * TQ:(h + 1) * TQ] = (ql * (HEAD_DIM ** -0.5)).astype(BF16)
    wt_ref[...] = wi_ref[0].astype(F32).T * (IDX_HEADS ** -0.5 * IDX_DIM ** -0.5)

    t_pos = qt * TQ + lax.broadcasted_iota(I32, (TS, TQ), 1)
    s_loc = lax.broadcasted_iota(I32, (TS, TQ), 0)

    def idx_body(j, carry):
        row0 = pl.multiple_of(j * TS, TS)
        kt = k_ref[0, pl.ds(row0, TS), :]
        score = jnp.zeros((TS, TQ), F32)
        for hp in range(IDX_HEADS // 2):
            sc = _nt_dot(kt, qm_ref[hp * 2 * TQ:(hp + 1) * 2 * TQ, :])
            for i in range(2):
                h = 2 * hp + i
                score = score + jnp.maximum(sc[:, i * TQ:(i + 1) * TQ], 0.0) * wt_ref[h:h + 1, :]
        bits = pltpu.bitcast(score, I32)
        key = bits ^ ((bits >> 31) & 0x7FFFFFFF)
        key = jnp.where(row0 + s_loc <= t_pos, key, INT_MIN)
        key_ref[pl.ds(row0, TS), :] = key
        return carry

    lax.fori_loop(0, n_s, idx_body, 0)

    def count(pred_fn):
        def body(j, part):
            row0 = pl.multiple_of(j * TS, TS)
            blk = key_ref[pl.ds(row0, TS), :]
            hit = pred_fn(blk, row0).astype(I32)
            return part + hit.reshape(TS // 8, 8, TQ).sum(axis=0)
        part = lax.fori_loop(0, n_s, body, jnp.zeros((8, TQ), I32))
        return part.sum(axis=0, keepdims=True)

    c0 = count(lambda blk, row0: blk >= 0)
    thr = jnp.where(c0 >= n_keep, jnp.zeros((1, TQ), I32), jnp.full((1, TQ), INT_MIN, I32))

    def bit_body(i, thr):
        cand = thr | (jnp.int32(1) << (30 - i))
        c = count(lambda blk, row0: blk >= cand)
        return jnp.where(c >= n_keep, cand, thr)

    thr = lax.fori_loop(0, 31, bit_body, thr)

    c_gt = count(lambda blk, row0: blk > thr)
    c_ge = count(lambda blk, row0: blk >= thr)
    need = n_keep - c_gt

    def tie_search():
        def tie_body(i, x):
            cand = x | (jnp.int32(1) << (idx_bits - 1 - i))
            c = count(lambda blk, row0: (blk == thr) & (row0 + s_loc < cand))
            return jnp.where(c < need, cand, x)
        return lax.fori_loop(0, idx_bits, tie_body, jnp.zeros((1, TQ), I32))

    tie_pos = lax.cond(jnp.max(c_ge) > n_keep, tie_search, lambda: jnp.full((1, TQ), 2 ** idx_bits - 1, I32))

    m_ref[...] = jnp.full(m_ref.shape, NEG_INF, F32)
    l_ref[...] = jnp.zeros(l_ref.shape, F32)
    acc_ref[...] = jnp.zeros(acc_ref.shape, F32)

    def apply_pending(j_prev, pend):
        ckv_t = ct_ref[0, j_prev]
        for hg in range(H // DSA_HG):
            gcols = slice(hg * DSA_HG * TQ, (hg + 1) * DSA_HG * TQ)
            pv = jnp.dot(ckv_t, p_ref[pend, :, gcols], preferred_element_type=F32)
            acc_ref[:, gcols] = acc_ref[:, gcols] * a_ref[pend, :, gcols] + pv

    def att_tile(j, near, pend):
        row0 = pl.multiple_of(j * TS, TS)
        key = key_ref[pl.ds(row0, TS), :]
        s_pos = row0 + s_loc
        sel = ((key > thr) | ((key == thr) & (s_pos <= tie_pos))) & (s_pos <= t_pos)
        ckv = c_ref[0, pl.ds(row0, TS), :]
        apply_pending(jnp.maximum(j - 1, 0), pend)
        for hg in range(H // DSA_HG):
            gcols = slice(hg * DSA_HG * TQ, (hg + 1) * DSA_HG * TQ)
            lg = jnp.dot(ckv, qlt_ref[:, gcols], preferred_element_type=F32)
            for i in range(DSA_HG):
                h = DSA_HG * hg + i
                cols = slice(h * TQ, (h + 1) * TQ)
                x = lg[:, i * TQ:(i + 1) * TQ]
                if near:
                    pieces = []
                    for sb in range(n_sub):
                        delta = qt - (j * n_sub + sb)
                        pieces.append(jnp.where(delta == 0, bias_ref[0, h],
                                                jnp.where(delta == 1, bias_ref[1, h], 0.0)))
                    x = x + (pieces[0] if n_sub == 1 else jnp.concatenate(pieces, axis=0))
                lh = jnp.where(sel, x, NEG_INF)
                m_old = m_ref[:, cols]
                m_new = jnp.maximum(m_old, jnp.max(lh, axis=0, keepdims=True))
                p = jnp.exp(lh - m_new)
                alpha = jnp.exp(m_old - m_new)
                l_ref[:, cols] = alpha * l_ref[:, cols] + jnp.sum(p, axis=0, keepdims=True)
                m_ref[:, cols] = m_new
                p_ref[1 - pend, :, cols] = p.astype(BF16)
                a_ref[1 - pend, :, cols] = alpha

    p_ref[1] = jnp.zeros(p_ref.shape[1:], BF16)
    a_ref[1] = jnp.ones(a_ref.shape[1:], F32)

    def pair_body(near):
        def body(i, carry):
            att_tile(2 * i, near, pend=1)
            att_tile(2 * i + 1, near, pend=0)
            return carry
        return body

    n_far_pairs = (jnp.maximum(qt - 1, 0) // n_sub) // 2
    n_pairs = n_s // 2
    lax.fori_loop(0, n_far_pairs, pair_body(False), 0)
    lax.fori_loop(n_far_pairs, n_pairs, pair_body(True), 0)

    @pl.when(n_s % 2 == 1)
    def _():
        att_tile(n_s - 1, True, pend=1)
        apply_pending(n_s - 1, 0)

    @pl.when(n_s % 2 == 0)
    def _():
        apply_pending(n_s - 1, 1)

    for h in range(H):
        cols = slice(h * TQ, (h + 1) * TQ)
        ol_t = (acc_ref[:, cols] * (1.0 / l_ref[:, cols])).astype(BF16)
        o_t = jnp.dot(wuvt_ref[h], ol_t, preferred_element_type=F32)
        o_ref[0, :, h * HEAD_DIM:(h + 1) * HEAD_DIM] = o_t.T.astype(o_ref.dtype)


def dsa_attention(u, ckv_n, ckv_t, w_uk2, w_uvt, rel_bias, cols, n_keep):
    B, S, _ = u.shape
    ts = ckv_t.shape[3]
    H, R = DSA_HEADS, KV_RANK
    qi_blk, q_blk, wi_blk, k_blk = cols
    kern = functools.partial(_dsa_kernel, n_keep=n_keep)
    return pl.pallas_call(
        kern,
        grid=(B, S // DSA_TQ),
        in_specs=[
            pl.BlockSpec(memory_space=pltpu.SMEM),
            pl.BlockSpec((1, DSA_TQ, IDX_HEADS * IDX_DIM), lambda b, t: (b, t, qi_blk)),
            pl.BlockSpec((1, DSA_TQ, H * HEAD_DIM), lambda b, t: (b, t, q_blk)),
            pl.BlockSpec((1, DSA_TQ, 128), lambda b, t: (b, t, wi_blk)),
            pl.BlockSpec((1, S, IDX_DIM), lambda b, t: (b, 0, k_blk)),
            pl.BlockSpec((1, S, R), lambda b, t: (b, 0, 0)),
            pl.BlockSpec((1, S // ts, R, ts), lambda b, t: (b, 0, 0, 0)),
            pl.BlockSpec((R, H * HEAD_DIM), lambda b, t: (0, 0)),
            pl.BlockSpec((H, HEAD_DIM, R), lambda b, t: (0, 0, 0)),
        ],
        out_specs=pl.BlockSpec((1, DSA_TQ, H * HEAD_DIM), lambda b, t: (b, t, 0)),
        out_shape=jax.ShapeDtypeStruct((B, S, H * HEAD_DIM), BF16),
        scratch_shapes=[
            pltpu.VMEM((IDX_HEADS * DSA_TQ, IDX_DIM), BF16),
            pltpu.VMEM((R, H * DSA_TQ), BF16),
            pltpu.VMEM((128, DSA_TQ), F32),
            pltpu.VMEM((S, DSA_TQ), I32),
            pltpu.VMEM((R, H * DSA_TQ), F32),
            pltpu.VMEM((1, H * DSA_TQ), F32),
            pltpu.VMEM((1, H * DSA_TQ), F32),
            pltpu.VMEM((2, H, SUB, DSA_TQ), F32),
            pltpu.VMEM((2, ts, H * DSA_TQ), BF16),
            pltpu.VMEM((2, 1, H * DSA_TQ), F32),
        ],
        compiler_params=pltpu.CompilerParams(
            dimension_semantics=("arbitrary", "arbitrary"),
            vmem_limit_bytes=VMEM_LIMIT_BYTES),
        name="dsa_attention",
    )(rel_bias, u, u, u, u, ckv_n, ckv_t, w_uk2, w_uvt)


CONV_ROWS = 256
CONV_HALO = 32
CONV_RC = 64
CONV_CW = 512


def _conv_taps(ext_ref, w_ref, y_ref, width, init_fn):
    ts, ch = y_ref.shape
    cw = min(CONV_CW, ch)
    n = CONV_RC + SUBLANES
    assert CONV_HALO >= SUBLANES * ((width - 1) // SUBLANES + 1)

    def body(i, carry):
        r0 = pl.multiple_of(i * CONV_RC, CONV_RC)
        for cc in range(ch // cw):
            cols = slice(cc * cw, (cc + 1) * cw)
            acc = init_fn(cols)
            for b in range(min(SUBLANES, width)):
                z = None
                for a in range((width - b + SUBLANES - 1) // SUBLANES):
                    j = SUBLANES * a + b
                    win = ext_ref[pl.ds(r0 + (CONV_HALO - SUBLANES * (a + 1)), n), cols]
                    term = win * w_ref[width - 1 - j:width - j, cols]
                    z = term if z is None else z + term
                if b == 0:
                    acc = acc + z[SUBLANES:, :]
                else:
                    acc = acc + pltpu.roll(z, n - (SUBLANES - b), axis=0)[:CONV_RC, :]
            y_ref[pl.ds(r0, CONV_RC), cols] = acc
        return carry

    lax.fori_loop(0, ts // CONV_RC, body, 0)


def _conv_gated_kernel(v_ref, g_ref, vp_ref, gp_ref, w_ref, cb_ref, ng_ref, nb_ref, o_ref, ext_ref, y_ref):
    t = pl.program_id(1)
    ts = v_ref.shape[1]
    width = w_ref.shape[0]
    prev = vp_ref[0].astype(F32) * jax.nn.sigmoid(gp_ref[0].astype(F32))
    ext_ref[0:CONV_HALO, :] = jnp.where(t > 0, prev, 0.0)

    def glu_body(i, carry):
        rows = pl.ds(pl.multiple_of(i * CONV_RC, CONV_RC), CONV_RC)
        ext_ref[pl.ds(pl.multiple_of(CONV_HALO + i * CONV_RC, CONV_HALO), CONV_RC), :] = (
            v_ref[0, rows, :].astype(F32) * jax.nn.sigmoid(g_ref[0, rows, :].astype(F32)))
        return carry

    lax.fori_loop(0, ts // CONV_RC, glu_body, 0)
    _conv_taps(ext_ref, w_ref, y_ref, width,
               lambda cols: jnp.broadcast_to(cb_ref[:, cols], (CONV_RC, cols.stop - cols.start)))
    def store(rows, z):
        o_ref[0, rows, :] = (z * jax.nn.sigmoid(z)).astype(o_ref.dtype)

    _layer_norm_loop(ts, lambda rows: y_ref[rows, :], store, ng_ref[...], nb_ref[...])


def _conv_short_kernel(gb_ref, gc_ref, h_ref, gcp_ref, hp_ref, w_ref, o_ref, ext_ref, y_ref):
    t = pl.program_id(1)
    ts = gb_ref.shape[1]
    width = w_ref.shape[0]
    prev = gcp_ref[0].astype(F32) * hp_ref[0].astype(F32)
    ext_ref[0:CONV_HALO, :] = jnp.where(t > 0, prev, 0.0)

    def mul_body(i, carry):
        rows = pl.ds(pl.multiple_of(i * CONV_RC, CONV_RC), CONV_RC)
        ext_ref[pl.ds(pl.multiple_of(CONV_HALO + i * CONV_RC, CONV_HALO), CONV_RC), :] = (
            gc_ref[0, rows, :].astype(F32) * h_ref[0, rows, :].astype(F32))
        return carry

    lax.fori_loop(0, ts // CONV_RC, mul_body, 0)
    _conv_taps(ext_ref, w_ref, y_ref, width,
               lambda cols: jnp.zeros((CONV_RC, cols.stop - cols.start), F32))

    def out_body(i, carry):
        rows = pl.ds(pl.multiple_of(i * CONV_RC, CONV_RC), CONV_RC)
        o_ref[0, rows, :] = (gb_ref[0, rows, :].astype(F32) * y_ref[rows, :]).astype(o_ref.dtype)
        return carry

    lax.fori_loop(0, ts // CONV_RC, out_body, 0)


def _halo_spec(ch, ts, col_block):
    per = ts // CONV_HALO
    return pl.BlockSpec((1, CONV_HALO, ch), lambda b, t: (b, jnp.maximum(t * per - 1, 0), col_block))


def conv_gated(u, val_blk, gate_blk, ch, conv_w, conv_b, norm_g, norm_b):
    B, S, _ = u.shape
    ts = min(CONV_ROWS, S)
    assert conv_w.shape[0] - 1 <= CONV_HALO and S % ts == 0
    cur = lambda blk: pl.BlockSpec((1, ts, ch), lambda b, t: (b, t, blk))
    vec = pl.BlockSpec((1, ch), lambda b, t: (0, 0))
    return pl.pallas_call(
        _conv_gated_kernel,
        grid=(B, S // ts),
        in_specs=[cur(val_blk), cur(gate_blk), _halo_spec(ch, ts, val_blk), _halo_spec(ch, ts, gate_blk),
                  pl.BlockSpec(conv_w.shape, lambda b, t: (0, 0)), vec, vec, vec],
        out_specs=pl.BlockSpec((1, ts, ch), lambda b, t: (b, t, 0)),
        out_shape=jax.ShapeDtypeStruct((B, S, ch), BF16),
        scratch_shapes=[pltpu.VMEM((CONV_HALO + ts, ch), F32), pltpu.VMEM((ts, ch), F32)],
        compiler_params=pltpu.CompilerParams(dimension_semantics=("parallel", "parallel"),
                                             vmem_limit_bytes=VMEM_LIMIT_BYTES),
        name="conv_gated",
    )(u, u, u, u, conv_w, conv_b.reshape(1, ch), norm_g.reshape(1, ch), norm_b.reshape(1, ch))


def conv_short(u, gb_blk, gc_blk, h_blk, ch, conv_w):
    B, S, _ = u.shape
    ts = min(CONV_ROWS, S)
    assert conv_w.shape[0] - 1 <= CONV_HALO and S % ts == 0
    cur = lambda blk: pl.BlockSpec((1, ts, ch), lambda b, t: (b, t, blk))
    return pl.pallas_call(
        _conv_short_kernel,
        grid=(B, S // ts),
        in_specs=[cur(gb_blk), cur(gc_blk), cur(h_blk), _halo_spec(ch, ts, gc_blk), _halo_spec(ch, ts, h_blk),
                  pl.BlockSpec(conv_w.shape, lambda b, t: (0, 0))],
        out_specs=pl.BlockSpec((1, ts, ch), lambda b, t: (b, t, 0)),
        out_shape=jax.ShapeDtypeStruct((B, S, ch), BF16),
        scratch_shapes=[pltpu.VMEM((CONV_HALO + ts, ch), F32), pltpu.VMEM((ts, ch), F32)],
        compiler_params=pltpu.CompilerParams(dimension_semantics=("parallel", "parallel"),
                                             vmem_limit_bytes=VMEM_LIMIT_BYTES),
        name="conv_short",
    )(u, u, u, u, u, conv_w)


def _dilated_kernel(relb_ref, q_ref, kp_ref, kc_ref, vp_ref, vc_ref, o_ref, lse_ref, bias_ref, *, dilation, steps):
    n = pl.program_id(2)
    BLK = DIL_BLOCK
    i_pos = lax.broadcasted_iota(I32, (BLK, 2 * BLK), 0)
    j_pos = lax.broadcasted_iota(I32, (BLK, 2 * BLK), 1)
    m = BLK + i_pos - j_pos

    @pl.when((pl.program_id(0) == 0) & (pl.program_id(1) == 0) & (n == 0))
    def _():
        for h in range(DIL_SLOTS):
            bias_ref[h] = _t5_bias_tile(relb_ref, h, m * dilation)

    valid = (m >= 0) & (m <= steps) & ((n > 0) | (j_pos >= BLK))
    for h in range(DIL_SLOTS):
        cols = slice(h * HEAD_DIM, (h + 1) * HEAD_DIM)
        q = q_ref[0, 0, :, cols]
        kb = jnp.concatenate([kp_ref[0, 0, :, cols], kc_ref[0, 0, :, cols]], axis=0)
        vb = jnp.concatenate([vp_ref[0, 0, :, cols], vc_ref[0, 0, :, cols]], axis=0)
        lg = _nt_dot(q, kb) * (HEAD_DIM ** -0.5) + bias_ref[h]
        lg = jnp.where(valid, lg, NEG_INF)
        mx = jnp.max(lg, axis=-1, keepdims=True)
        p = jnp.exp(lg - mx)
        l = jnp.sum(p, axis=-1, keepdims=True)
        o = jnp.dot(p.astype(BF16), vb, preferred_element_type=F32)
        o_ref[0, 0, :, cols] = o * (1.0 / l)
        lse_ref[0, 0, :, cols] = jnp.broadcast_to(mx + jnp.log(l), (BLK, HEAD_DIM))


def dilated_group(qkv, rel_bias, q_blk, k_blk, v_blk, window):
    B, dilation, L, N = qkv.shape
    W = DIL_SLOTS * HEAD_DIM
    assert L % DIL_BLOCK == 0 and N % W == 0 and window // dilation <= DIL_BLOCK
    cur = lambda blk: pl.BlockSpec((1, 1, DIL_BLOCK, W), lambda b, r, n: (b, r, n, blk))
    prev = lambda blk: pl.BlockSpec((1, 1, DIL_BLOCK, W), lambda b, r, n: (b, r, jnp.maximum(n - 1, 0), blk))
    out = pl.BlockSpec((1, 1, DIL_BLOCK, W), lambda b, r, n: (b, r, n, 0))
    kern = functools.partial(_dilated_kernel, dilation=dilation, steps=window // dilation)
    return pl.pallas_call(
        kern,
        grid=(B, dilation, L // DIL_BLOCK),
        in_specs=[pl.BlockSpec(memory_space=pltpu.SMEM), cur(q_blk), prev(k_blk), cur(k_blk), prev(v_blk), cur(v_blk)],
        out_specs=[out, out],
        out_shape=[jax.ShapeDtypeStruct((B, dilation, L, W), F32)] * 2,
        scratch_shapes=[pltpu.VMEM((DIL_SLOTS, DIL_BLOCK, 2 * DIL_BLOCK), F32)],
        compiler_params=pltpu.CompilerParams(dimension_semantics=("arbitrary", "arbitrary", "arbitrary")),
        name=f"dilated_d{dilation}",
    )(rel_bias, qkv, qkv, qkv, qkv, qkv)


def _dil_combine_kernel(*refs, dilations):
    g = len(dilations)
    o_refs, l_refs, out_ref, tmp_refs = refs[:g], refs[g:2 * g], refs[2 * g], refs[2 * g + 1:]

    def in_position_order(ref, tmp_ref, d):
        if d == 1:
            return ref[0, 0]
        rows = ref.shape[2]
        for c in range(tmp_ref.shape[0]):
            for r in range(d):
                tmp_ref[c, pl.ds(r, rows, stride=d), :] = ref[0, r, :, c * LANES:(c + 1) * LANES]
        return jnp.concatenate([tmp_ref[c] for c in range(tmp_ref.shape[0])], axis=1)

    ls = [in_position_order(l_refs[i], tmp_refs[2 * i], d) for i, d in enumerate(dilations)]
    mx = functools.reduce(jnp.maximum, ls)
    es = [jnp.exp(l - mx) for l in ls]
    den = functools.reduce(jnp.add, es)
    num = None
    for i, d in enumerate(dilations):
        term = es[i] * in_position_order(o_refs[i], tmp_refs[2 * i + 1], d)
        num = term if num is None else num + term
    out_ref[0] = (num / den).astype(out_ref.dtype)


DIL_COMBINE_ROWS = 512


def dilated_combine(outs, lses):
    B, _, _, W = outs[0].shape
    dilations = tuple(o.shape[1] for o in outs)
    S = outs[0].shape[1] * outs[0].shape[2]
    ts = min(DIL_COMBINE_ROWS, S)
    assert all(ts % (d * SUBLANES) == 0 for d in dilations)
    specs = [pl.BlockSpec((1, d, ts // d, W), lambda b, t: (b, 0, t, 0)) for d in dilations]
    return pl.pallas_call(
        functools.partial(_dil_combine_kernel, dilations=dilations),
        grid=(B, S // ts),
        in_specs=specs + specs,
        out_specs=pl.BlockSpec((1, ts, W), lambda b, t: (b, t, 0)),
        out_shape=jax.ShapeDtypeStruct((B, S, W), BF16),
        scratch_shapes=[pltpu.VMEM((W // LANES, ts, LANES), F32)] * (2 * len(dilations)),
        compiler_params=pltpu.CompilerParams(dimension_semantics=("parallel", "parallel"),
                                             vmem_limit_bytes=VMEM_LIMIT_BYTES),
        name="dilated_combine",
    )(*outs, *lses)


EVEN_ATTN_COLS = 6144
EVEN_QIDX_COL = 0
EVEN_Q_COL = IDX_HEADS * IDX_DIM
EVEN_CKV_COL = EVEN_Q_COL + DSA_HEADS * HEAD_DIM
EVEN_KIDX_COL = EVEN_CKV_COL + KV_RANK
EVEN_WIDX_COL = EVEN_KIDX_COL + IDX_DIM


def _even_col_tiles(ca):
    q0 = 2 * ca
    ckv0 = q0 + DSA_HEADS * HEAD_DIM
    qidx0 = ckv0 + KV_RANK
    kidx0 = qidx0 + IDX_HEADS * IDX_DIM
    tiles = _col_tiles((qidx0, IDX_HEADS * IDX_DIM), (q0, DSA_HEADS * HEAD_DIM), (ckv0, KV_RANK),
                       (kidx0, PROJ_COLS), (0, 2 * ca))
    assert len(tiles) * PROJ_COLS == EVEN_ATTN_COLS + 2 * ca
    return tiles


def mixer_even(xs, w_in, conv_w, conv_b, norm_g, norm_b, kv_norm_g, w_uk, w_uv, rel_bias, B, S):
    ca = conv_w.shape[-1]
    assert EVEN_ATTN_COLS % ca == 0
    w_b = w_in.astype(BF16)
    w_b = jnp.pad(w_b, ((0, 0), (0, (-w_b.shape[1]) % PROJ_COLS)))
    u = project(xs, w_b, _even_col_tiles(ca), S).reshape(B, S, -1)
    a = conv_gated(u, EVEN_ATTN_COLS // ca, EVEN_ATTN_COLS // ca + 1, ca, conv_w, conv_b, norm_g, norm_b)
    cn, ct = ckv_norm(u, kv_norm_g.reshape(1, KV_RANK), EVEN_CKV_COL // KV_RANK, min(DSA_TS, S))
    o = dsa_attention(u, cn, ct, w_uk.reshape(KV_RANK, DSA_HEADS * HEAD_DIM).astype(BF16),
                      jnp.transpose(w_uv, (1, 2, 0)).astype(BF16), rel_bias,
                      (EVEN_QIDX_COL // (IDX_HEADS * IDX_DIM), EVEN_Q_COL // (DSA_HEADS * HEAD_DIM),
                       EVEN_WIDX_COL // 128, EVEN_KIDX_COL // IDX_DIM), min(IDX_TOPK, S // 4))
    return a, o


def mixer_odd(xs, w_in, conv_w, rel_bias, B, S):
    cc = conv_w.shape[-1]
    W = DIL_SLOTS * HEAD_DIM
    assert (3 * cc) % W == 0 and DIL_CONFIGS[0][1] == 1
    w_b = w_in.astype(BF16)
    qkv_tiles = lambda g: [(3 * cc + (i * DIL_GROUPS + g) * W, W) for i in range(3)]
    u = project(xs, w_b, _col_tiles((0, 3 * cc), *qkv_tiles(0)), S).reshape(B, S, -1)
    c_out = conv_short(u, 0, 1, 2, cc, conv_w)
    q_blk = 3 * cc // W
    results = [dilated_group(u.reshape(B, 1, S, -1), rel_bias, q_blk, q_blk + 1, q_blk + 2, DIL_CONFIGS[0][0])]
    for g, (window, dilation) in list(enumerate(DIL_CONFIGS))[1:]:
        qkv = project(xs, w_b, _col_tiles(*qkv_tiles(g)), S, dilation)
        results.append(dilated_group(qkv, rel_bias, 0, 1, 2, window))
    o = dilated_combine([r[0] for r in results], [r[1] for r in results])
    return c_out, o


def kernel(x, w_ffn_in, w_ffn_out, post_ln_g, post_ln_b, rel_bias, w_in_e, w_out_e, conv_a_w, conv_a_b,
           conv_a_norm_g, conv_a_norm_b, kv_norm_g, w_uk, w_uv, w_in_o, w_out_o, conv_c_w):
    B, S, D = x.shape
    M = B * S
    depth = w_ffn_in.shape[0]
    xs = x.reshape(M, D)
    w_ffn_in_b = w_ffn_in.astype(BF16)
    w_ffn_out_b = w_ffn_out.astype(BF16)
    for layer in range(depth):
        g = post_ln_g[layer].reshape(3, 1, D)
        b = post_ln_b[layer].reshape(3, 1, D)
        xs = ffn_ln(xs, w_ffn_in_b[layer, 0], w_ffn_out_b[layer, 0], g[0], b[0])
        if layer % 2 == 0:
            e = layer // 2
            conv_out, attn_out = mixer_even(xs, w_in_e[e], conv_a_w[e], conv_a_b[e], conv_a_norm_g[e],
                                            conv_a_norm_b[e], kv_norm_g[e], w_uk[e], w_uv[e], rel_bias, B, S)
            w_out = w_out_e[e]
        else:
            o = layer // 2
            conv_out, attn_out = mixer_odd(xs, w_in_o[o], conv_c_w[o], rel_bias, B, S)
            w_out = w_out_o[o]
        xs = out_ln(conv_out.reshape(M, -1), attn_out.reshape(M, -1), w_out.astype(BF16), xs, g[1], b[1])
        xs = ffn_ln(xs, w_ffn_in_b[layer, 1], w_ffn_out_b[layer, 1], g[2], b[2])
    return xs.reshape(B, S, D)
```

```python
import functools
import math

import jax
import jax.numpy as jnp
import numpy as np
from jax import lax
from jax.experimental import pallas as pl
from jax.experimental.pallas import tpu as pltpu

DEPTH = 2
ALPHA = (2 * DEPTH) ** 0.25
LN_EPS = 1e-5
NEG_INF = -1e30

HEAD_DIM = 128
DSA_HEADS = 8
KV_RANK = 512
IDX_HEADS = 32
IDX_DIM = 128
IDX_TOPK = 256
DIL_SLOTS = 8
DIL_CONFIGS = ((128, 1), (512, 4), (2048, 16))
DIL_GROUPS = len(DIL_CONFIGS)
DIL_HEADS = DIL_SLOTS * DIL_GROUPS
DIL_BLOCK = 128
REL_BUCKETS = 32
REL_MAX_DIST = 128

V7X_VMEM_BYTES = 64 * 1024 * 1024
VMEM_LIMIT_BYTES = V7X_VMEM_BYTES - 6 * 1024 * 1024

BF16 = jnp.bfloat16
F32 = jnp.float32
I32 = jnp.int32
INT_MIN = -2 ** 31
SUBLANES = 8
LANES = 128


def _layer_norm_rows(y, g, b):
    mu = jnp.mean(y, axis=-1, keepdims=True)
    yc = y - mu
    var = jnp.mean(yc * yc, axis=-1, keepdims=True)
    return yc * lax.rsqrt(var + LN_EPS) * g + b


LN_CHUNK = 16
LN_GROUP = 4


def _layer_norm_loop(n_rows, load_fn, store_fn, g, b):
    step = LN_CHUNK * LN_GROUP
    assert n_rows % step == 0

    def body(i, carry):
        base = pl.multiple_of(i * step, step)
        rows = [pl.ds(base + c * LN_CHUNK, LN_CHUNK) for c in range(LN_GROUP)]
        ys = [load_fn(r) for r in rows]
        for r, y in zip(rows, ys):
            store_fn(r, _layer_norm_rows(y, g, b))
        return carry

    lax.fori_loop(0, n_rows // step, body, 0)


def _nt_dot(a, b):
    return lax.dot_general(a, b, (((1,), (1,)), ((), ())), preferred_element_type=F32)


FFN_ROWS = 512
FFN_HIDDEN = 256


def _ffn_ln_kernel(x_ref, wg_ref, wu_ref, wo_ref, g_ref, b_ref, o_ref, xb_ref, h_ref):
    f = pl.program_id(1)
    nf = pl.num_programs(1) - 1

    def hidden_tile():
        xb = xb_ref[...]
        gate = jnp.dot(xb, wg_ref[...], preferred_element_type=F32)
        up = jnp.dot(xb, wu_ref[...], preferred_element_type=F32)
        return (gate * jax.nn.sigmoid(gate) * up).astype(BF16)

    @pl.when(f == 0)
    def _():
        xb_ref[...] = x_ref[...].astype(BF16)
        o_ref[...] = jnp.zeros_like(o_ref)
        h_ref[0] = hidden_tile()

    @pl.when((f > 0) & (f < nf))
    def _():
        o_ref[...] += jnp.dot(h_ref[(f - 1) % 2], wo_ref[...], preferred_element_type=F32)
        h_ref[f % 2] = hidden_tile()

    @pl.when(f == nf)
    def _():
        o_ref[...] += jnp.dot(h_ref[(f - 1) % 2], wo_ref[...], preferred_element_type=F32)

        def store(rows, z):
            o_ref[rows, :] = z

        _layer_norm_loop(x_ref.shape[0], lambda rows: ALPHA * x_ref[rows, :] + 0.5 * o_ref[rows, :], store,
                         g_ref[...], b_ref[...])


def ffn_ln(x, w_in, w_out, layer, idx, g, b):
    m, d = x.shape
    f = w_out.shape[2]
    tm = min(FFN_ROWS, m)
    tf = FFN_HIDDEN
    assert m % tm == 0 and f % tf == 0
    nf = f // tf
    return pl.pallas_call(
        _ffn_ln_kernel,
        grid=(m // tm, nf + 1),
        in_specs=[
            pl.BlockSpec((tm, d), lambda i, j: (i, 0)),
            pl.BlockSpec((None, None, d, tf), lambda i, j: (layer, idx, 0, jnp.minimum(j, nf - 1))),
            pl.BlockSpec((None, None, d, tf), lambda i, j: (layer, idx, 0, jnp.minimum(j, nf - 1) + nf)),
            pl.BlockSpec((None, None, tf, d), lambda i, j: (layer, idx, jnp.maximum(j - 1, 0), 0)),
            pl.BlockSpec((1, d), lambda i, j: (0, 0)),
            pl.BlockSpec((1, d), lambda i, j: (0, 0)),
        ],
        out_specs=pl.BlockSpec((tm, d), lambda i, j: (i, 0)),
        out_shape=jax.ShapeDtypeStruct((m, d), F32),
        scratch_shapes=[pltpu.VMEM((tm, d), BF16), pltpu.VMEM((2, tm, tf), BF16)],
        compiler_params=pltpu.CompilerParams(
            dimension_semantics=("parallel", "arbitrary"),
            vmem_limit_bytes=VMEM_LIMIT_BYTES),
        name="ffn_ln",
    )(x, w_in, w_in, w_out, g, b)


PROJ_ROWS = 1024
PROJ_COLS = 512


def _proj_kernel(tiles_ref, x_ref, w_ref, o_ref, xb_ref, *res_ref, dilation):
    del tiles_ref

    @pl.when(pl.program_id(1) == 0)
    def _():
        xb_ref[...] = x_ref[...].astype(BF16)

    y = jnp.dot(xb_ref[...], w_ref[...], preferred_element_type=F32)
    if dilation == 1:
        o_ref[...] = y.astype(o_ref.dtype)
    else:
        y_ref, = res_ref
        rows = y_ref.shape[1] // dilation
        for c in range(y_ref.shape[0]):
            lanes = slice(c * LANES, (c + 1) * LANES)
            y_ref[c] = y[:, lanes]
            for r in range(dilation):
                o_ref[0, r, :, lanes] = y_ref[c, pl.ds(r, rows, stride=dilation), :].astype(o_ref.dtype)


def _col_tiles(*pieces):
    tiles = []
    for start, width in pieces:
        assert start % PROJ_COLS == 0 and width % PROJ_COLS == 0
        tiles += list(range(start // PROJ_COLS, (start + width) // PROJ_COLS))
    return np.asarray(tiles, np.int32)


def project(x, w, col_tiles, seq_len, dilation=1):
    m, k = x.shape
    tm = min(PROJ_ROWS, seq_len)
    tn = PROJ_COLS
    n = tn * len(col_tiles)
    assert m % seq_len == 0 and seq_len % tm == 0 and w.shape[1] % tn == 0 and tm % (dilation * 16) == 0
    if dilation == 1:
        out_spec = pl.BlockSpec((tm, tn), lambda i, j, tiles: (i, j))
        out_shape = jax.ShapeDtypeStruct((m, n), BF16)
        scratch = []
    else:
        per = seq_len // tm
        out_spec = pl.BlockSpec((1, dilation, tm // dilation, tn), lambda i, j, tiles: (i // per, 0, i % per, j))
        out_shape = jax.ShapeDtypeStruct((m // seq_len, dilation, seq_len // dilation, n), BF16)
        scratch = [pltpu.VMEM((tn // LANES, tm, LANES), F32)]
    return pl.pallas_call(
        functools.partial(_proj_kernel, dilation=dilation),
        grid_spec=pltpu.PrefetchScalarGridSpec(
            num_scalar_prefetch=1,
            grid=(m // tm, len(col_tiles)),
            in_specs=[
                pl.BlockSpec((tm, k), lambda i, j, tiles: (i, 0)),
                pl.BlockSpec((k, tn), lambda i, j, tiles: (0, tiles[j])),
            ],
            out_specs=out_spec,
            scratch_shapes=[pltpu.VMEM((tm, k), BF16)] + scratch,
        ),
        out_shape=out_shape,
        compiler_params=pltpu.CompilerParams(
            dimension_semantics=("parallel", "arbitrary"),
            vmem_limit_bytes=VMEM_LIMIT_BYTES),
        name="project",
    )(jnp.asarray(col_tiles), x, w)


OUT_ROWS = 512
OUT_K = 1024


def _out_ln_kernel(a_ref, a2_ref, w_ref, x_ref, g_ref, b_ref, o_ref, *, n_first):
    kk = pl.program_id(1)

    @pl.when(kk == 0)
    def _():
        o_ref[...] = jnp.zeros_like(o_ref)

    @pl.when(kk < n_first)
    def _():
        o_ref[...] += jnp.dot(a_ref[...], w_ref[...], preferred_element_type=F32)

    @pl.when(kk >= n_first)
    def _():
        o_ref[...] += jnp.dot(a2_ref[...], w_ref[...], preferred_element_type=F32)

    @pl.when(kk == pl.num_programs(1) - 1)
    def _():
        def store(rows, z):
            o_ref[rows, :] = z

        _layer_norm_loop(x_ref.shape[0], lambda rows: ALPHA * x_ref[rows, :] + o_ref[rows, :], store,
                         g_ref[...], b_ref[...])


def out_ln(a, a2, w, x, g, b):
    m, k1 = a.shape
    k2 = a2.shape[1]
    d = w.shape[1]
    tm = min(OUT_ROWS, m)
    tk = min(OUT_K, k1, k2)
    assert m % tm == 0 and k1 % tk == 0 and k2 % tk == 0 and w.shape[0] == k1 + k2
    n1, n2 = k1 // tk, k2 // tk
    return pl.pallas_call(
        functools.partial(_out_ln_kernel, n_first=n1),
        grid=(m // tm, n1 + n2),
        in_specs=[
            pl.BlockSpec((tm, tk), lambda i, j: (i, jnp.minimum(j, n1 - 1))),
            pl.BlockSpec((tm, tk), lambda i, j: (i, jnp.maximum(j - n1, 0))),
            pl.BlockSpec((tk, d), lambda i, j: (j, 0)),
            pl.BlockSpec((tm, d), lambda i, j: (i, 0)),
            pl.BlockSpec((1, d), lambda i, j: (0, 0)),
            pl.BlockSpec((1, d), lambda i, j: (0, 0)),
        ],
        out_specs=pl.BlockSpec((tm, d), lambda i, j: (i, 0)),
        out_shape=jax.ShapeDtypeStruct((m, d), F32),
        compiler_params=pltpu.CompilerParams(
            dimension_semantics=("parallel", "arbitrary"),
            vmem_limit_bytes=VMEM_LIMIT_BYTES),
        name="out_ln",
    )(a, a2, w, x, g, b)


def _t5_thresholds():
    d = np.arange(0, 2 * REL_MAX_DIST)
    max_exact = REL_BUCKETS // 2
    nf = np.maximum(d, 1).astype(np.float32)
    large = max_exact + (np.log(nf / np.float32(max_exact)) / np.float32(math.log(REL_MAX_DIST / max_exact))
                         * np.float32(REL_BUCKETS - max_exact)).astype(np.int32)
    large = np.minimum(large, REL_BUCKETS - 1)
    bucket = np.where(d < max_exact, d, large)
    return [int(np.argmax(bucket >= k)) for k in range(REL_BUCKETS)]


T5_THR = _t5_thresholds()


def _t5_bias_tile(relb_ref, head, dist):
    acc = jnp.full(dist.shape, relb_ref[0, head], F32)
    for k in range(1, REL_BUCKETS):
        acc = jnp.where(dist >= T5_THR[k], relb_ref[k, head], acc)
    return acc


DSA_TQ = 128
DSA_TS = 512
DSA_HG = 2
SUB = 128
assert T5_THR[REL_BUCKETS - 1] <= SUB


def _ckv_kernel(c_ref, g_ref, cn_ref, ct_ref):
    c = c_ref[0].astype(F32)
    y = c * lax.rsqrt(jnp.mean(c * c, axis=-1, keepdims=True) + LN_EPS) * g_ref[...]
    cn_ref[0] = y.astype(BF16)
    ct_ref[0, 0] = y.T.astype(BF16)


def ckv_norm(u, g, col_block, ts):
    B, S, _ = u.shape
    R = KV_RANK
    return pl.pallas_call(
        _ckv_kernel,
        grid=(B, S // ts),
        in_specs=[pl.BlockSpec((1, ts, R), lambda b, j: (b, j, col_block)),
                  pl.BlockSpec((1, R), lambda b, j: (0, 0))],
        out_specs=[pl.BlockSpec((1, ts, R), lambda b, j: (b, j, 0)),
                   pl.BlockSpec((1, 1, R, ts), lambda b, j: (b, j, 0, 0))],
        out_shape=[jax.ShapeDtypeStruct((B, S, R), BF16),
                   jax.ShapeDtypeStruct((B, S // ts, R, ts), BF16)],
        compiler_params=pltpu.CompilerParams(dimension_semantics=("parallel", "parallel")),
        name="ckv_norm",
    )(u, g)


def _dsa_kernel(relb_ref, qi_ref, q_ref, wi_ref, k_ref, c_ref, ct_ref, wuk_ref, wuvt_ref, o_ref,
                qm_ref, qlt_ref, wt_ref, key_ref, acc_ref, m_ref, l_ref, bias_ref, p_ref, a_ref, *, n_keep):
    b = pl.program_id(0)
    qt = pl.program_id(1)
    S = k_ref.shape[1]
    TS = ct_ref.shape[3]
    TQ = DSA_TQ
    H = DSA_HEADS
    n_sub = TS // SUB
    n_s = (qt * TQ + TQ + TS - 1) // TS
    idx_bits = int(S).bit_length()

    @pl.when((b == 0) & (qt == 0))
    def _():
        s_l = lax.broadcasted_iota(I32, (SUB, TQ), 0)
        t_l = lax.broadcasted_iota(I32, (SUB, TQ), 1)
        for which in range(2):
            for h in range(H):
                bias_ref[which, h] = (_t5_bias_tile(relb_ref, h, t_l - s_l + SUB * which)
                                      - relb_ref[REL_BUCKETS - 1, h])

    for h in range(IDX_HEADS):
        qm_ref[h * TQ:(h + 1) * TQ, :] = qi_ref[0, :, h * IDX_DIM:(h + 1) * IDX_DIM]
    for h in range(H):
        ql = _nt_dot(wuk_ref[:, h * HEAD_DIM:(h + 1) * HEAD_DIM], q_ref[0, :, h * HEAD_DIM:(h + 1) * HEAD_DIM])
        qlt_ref[:, h * TQ:(h + 1) * TQ] = (ql * (HEAD_DIM ** -0.5)).astype(BF16)
    wt_ref[...] = wi_ref[0].astype(F32).T * (IDX_HEADS ** -0.5 * IDX_DIM ** -0.5)

    t_pos = qt * TQ + lax.broadcasted_iota(I32, (TS, TQ), 1)
    s_loc = lax.broadcasted_iota(I32, (TS, TQ), 0)

    def idx_body(j, carry):
        row0 = pl.multiple_of(j * TS, TS)
        kt = k_ref[0, pl.ds(row0, TS), :]
        score = jnp.zeros((TS, TQ), F32)
        for hp in range(IDX_HEADS // 2):
            sc = _nt_dot(kt, qm_ref[hp * 2 * TQ:(hp + 1) * 2 * TQ, :])
            for i in range(2):
                h = 2 * hp + i
                score = score + jnp.maximum(sc[:, i * TQ:(i + 1) * TQ], 0.0) * wt_ref[h:h + 1, :]
        bits = pltpu.bitcast(score, I32)
        key = bits ^ ((bits >> 31) & 0x7FFFFFFF)
        key = jnp.where(row0 + s_loc <= t_pos, key, INT_MIN)
        key_ref[pl.ds(row0, TS), :] = key
        return carry

    lax.fori_loop(0, n_s, idx_body, 0)

    def count(pred_fn):
        def body(j, part):
            row0 = pl.multiple_of(j * TS, TS)
            blk = key_ref[pl.ds(row0, TS), :]
            hit = pred_fn(blk, row0).astype(I32)
            return part + hit.reshape(TS // 8, 8, TQ).sum(axis=0)
        part = lax.fori_loop(0, n_s, body, jnp.zeros((8, TQ), I32))
        return part.sum(axis=0, keepdims=True)

    c0 = count(lambda blk, row0: blk >= 0)
    thr = jnp.where(c0 >= n_keep, jnp.zeros((1, TQ), I32), jnp.full((1, TQ), INT_MIN, I32))

    def bit_body(i, thr):
        cand = thr | (jnp.int32(1) << (30 - i))
        c = count(lambda blk, row0: blk >= cand)
        return jnp.where(c >= n_keep, cand, thr)

    thr = lax.fori_loop(0, 31, bit_body, thr)

    c_gt = count(lambda blk, row0: blk > thr)
    c_ge = count(lambda blk, row0: blk >= thr)
    need = n_keep - c_gt

    def tie_search():
        def tie_body(i, x):
            cand = x | (jnp.int32(1) << (idx_bits - 1 - i))
            c = count(lambda blk, row0: (blk == thr) & (row0 + s_loc < cand))
            return jnp.where(c < need, cand, x)
        return lax.fori_loop(0, idx_bits, tie_body, jnp.zeros((1, TQ), I32))

    tie_pos = lax.cond(jnp.max(c_ge) > n_keep, tie_search, lambda: jnp.full((1, TQ), 2 ** idx_bits - 1, I32))

    m_ref[...] = jnp.full(m_ref.shape, NEG_INF, F32)
    l_ref[...] = jnp.zeros(l_ref.shape, F32)
    acc_ref[...] = jnp.zeros(acc_ref.shape, F32)

    def apply_pending(j_prev, pend):
        ckv_t = ct_ref[0, j_prev]
        for hg in range(H // DSA_HG):
            gcols = slice(hg * DSA_HG * TQ, (hg + 1) * DSA_HG * TQ)
            pv = jnp.dot(ckv_t, p_ref[pend, :, gcols], preferred_element_type=F32)
            acc_ref[:, gcols] = acc_ref[:, gcols] * a_ref[pend, :, gcols] + pv

    def att_tile(j, near, pend):
        row0 = pl.multiple_of(j * TS, TS)
        key = key_ref[pl.ds(row0, TS), :]
        s_pos = row0 + s_loc
        sel = ((key > thr) | ((key == thr) & (s_pos <= tie_pos))) & (s_pos <= t_pos)
        ckv = c_ref[0, pl.ds(row0, TS), :]
        apply_pending(jnp.maximum(j - 1, 0), pend)
        for hg in range(H // DSA_HG):
            gcols = slice(hg * DSA_HG * TQ, (hg + 1) * DSA_HG * TQ)
            lg = jnp.dot(ckv, qlt_ref[:, gcols], preferred_element_type=F32)
            for i in range(DSA_HG):
                h = DSA_HG * hg + i
                cols = slice(h * TQ, (h + 1) * TQ)
                x = lg[:, i * TQ:(i + 1) * TQ]
                if near:
                    pieces = []
                    for sb in range(n_sub):
                        delta = qt - (j * n_sub + sb)
                        pieces.append(jnp.where(delta == 0, bias_ref[0, h],
                                                jnp.where(delta == 1, bias_ref[1, h], 0.0)))
                    x = x + (pieces[0] if n_sub == 1 else jnp.concatenate(pieces, axis=0))
                lh = jnp.where(sel, x, NEG_INF)
                m_old = m_ref[:, cols]
                m_new = jnp.maximum(m_old, jnp.max(lh, axis=0, keepdims=True))
                p = jnp.exp(lh - m_new)
                alpha = jnp.exp(m_old - m_new)
                l_ref[:, cols] = alpha * l_ref[:, cols] + jnp.sum(p, axis=0, keepdims=True)
                m_ref[:, cols] = m_new
                p_ref[1 - pend, :, cols] = p.astype(BF16)
                a_ref[1 - pend, :, cols] = alpha

    p_ref[1] = jnp.zeros(p_ref.shape[1:], BF16)
    a_ref[1] = jnp.ones(a_ref.shape[1:], F32)

    def pair_body(near):
        def body(i, carry):
            att_tile(2 * i, near, pend=1)
            att_tile(2 * i + 1, near, pend=0)
            return carry
        return body

    n_far_pairs = (jnp.maximum(qt - 1, 0) // n_sub) // 2
    n_pairs = n_s // 2
    lax.fori_loop(0, n_far_pairs, pair_body(False), 0)
    lax.fori_loop(n_far_pairs, n_pairs, pair_body(True), 0)

    @pl.when(n_s % 2 == 1)
    def _():
        att_tile(n_s - 1, True, pend=1)
        apply_pending(n_s - 1, 0)

    @pl.when(n_s % 2 == 0)
    def _():
        apply_pending(n_s - 1, 1)

    for h in range(H):
        cols = slice(h * TQ, (h + 1) * TQ)
        ol_t = (acc_ref[:, cols] * (1.0 / l_ref[:, cols])).astype(BF16)
        o_t = jnp.dot(wuvt_ref[h], ol_t, preferred_element_type=F32)
        o_ref[0, :, h * HEAD_DIM:(h + 1) * HEAD_DIM] = o_t.T.astype(o_ref.dtype)


def dsa_attention(u, ckv_n, ckv_t, w_uk2, w_uvt, rel_bias, cols, n_keep):
    B, S, _ = u.shape
    ts = ckv_t.shape[3]
    H, R = DSA_HEADS, KV_RANK
    qi_blk, q_blk, wi_blk, k_blk = cols
    kern = functools.partial(_dsa_kernel, n_keep=n_keep)
    return pl.pallas_call(
        kern,
        grid=(B, S // DSA_TQ),
        in_specs=[
            pl.BlockSpec(memory_space=pltpu.SMEM),
            pl.BlockSpec((1, DSA_TQ, IDX_HEADS * IDX_DIM), lambda b, t: (b, t, qi_blk)),
            pl.BlockSpec((1, DSA_TQ, H * HEAD_DIM), lambda b, t: (b, t, q_blk)),
            pl.BlockSpec((1, DSA_TQ, 128), lambda b, t: (b, t, wi_blk)),
            pl.BlockSpec((1, S, IDX_DIM), lambda b, t: (b, 0, k_blk)),
            pl.BlockSpec((1, S, R), lambda b, t: (b, 0, 0)),
            pl.BlockSpec((1, S // ts, R, ts), lambda b, t: (b, 0, 0, 0)),
            pl.BlockSpec((R, H * HEAD_DIM), lambda b, t: (0, 0)),
            pl.BlockSpec((H, HEAD_DIM, R), lambda b, t: (0, 0, 0)),
        ],
        out_specs=pl.BlockSpec((1, DSA_TQ, H * HEAD_DIM), lambda b, t: (b, t, 0)),
        out_shape=jax.ShapeDtypeStruct((B, S, H * HEAD_DIM), BF16),
        scratch_shapes=[
            pltpu.VMEM((IDX_HEADS * DSA_TQ, IDX_DIM), BF16),
            pltpu.VMEM((R, H * DSA_TQ), BF16),
            pltpu.VMEM((128, DSA_TQ), F32),
            pltpu.VMEM((S, DSA_TQ), I32),
            pltpu.VMEM((R, H * DSA_TQ), F32),
            pltpu.VMEM((1, H * DSA_TQ), F32),
            pltpu.VMEM((1, H * DSA_TQ), F32),
            pltpu.VMEM((2, H, SUB, DSA_TQ), F32),
            pltpu.VMEM((2, ts, H * DSA_TQ), BF16),
            pltpu.VMEM((2, 1, H * DSA_TQ), F32),
        ],
        compiler_params=pltpu.CompilerParams(
            dimension_semantics=("arbitrary", "arbitrary"),
            vmem_limit_bytes=VMEM_LIMIT_BYTES),
        name="dsa_attention",
    )(rel_bias, u, u, u, u, ckv_n, ckv_t, w_uk2, w_uvt)


CONV_ROWS = 256
CONV_HALO = 32
CONV_RC = 64
CONV_CW = 512


def _conv_taps(ext_ref, w_ref, y_ref, width, init_fn):
    ts, ch = y_ref.shape
    cw = min(CONV_CW, ch)
    n = CONV_RC + SUBLANES
    assert CONV_HALO >= SUBLANES * ((width - 1) // SUBLANES + 1)

    def body(i, carry):
        r0 = pl.multiple_of(i * CONV_RC, CONV_RC)
        for cc in range(ch // cw):
            cols = slice(cc * cw, (cc + 1) * cw)
            acc = init_fn(cols)
            for b in range(min(SUBLANES, width)):
                z = None
                for a in range((width - b + SUBLANES - 1) // SUBLANES):
                    j = SUBLANES * a + b
                    win = ext_ref[pl.ds(r0 + (CONV_HALO - SUBLANES * (a + 1)), n), cols]
                    term = win * w_ref[width - 1 - j:width - j, cols]
                    z = term if z is None else z + term
                if b == 0:
                    acc = acc + z[SUBLANES:, :]
                else:
                    acc = acc + pltpu.roll(z, n - (SUBLANES - b), axis=0)[:CONV_RC, :]
            y_ref[pl.ds(r0, CONV_RC), cols] = acc
        return carry

    lax.fori_loop(0, ts // CONV_RC, body, 0)


def _conv_gated_kernel(v_ref, g_ref, vp_ref, gp_ref, w_ref, cb_ref, ng_ref, nb_ref, o_ref, ext_ref, y_ref):
    t = pl.program_id(1)
    ts = v_ref.shape[1]
    width = w_ref.shape[0]
    prev = vp_ref[0].astype(F32) * jax.nn.sigmoid(gp_ref[0].astype(F32))
    ext_ref[0:CONV_HALO, :] = jnp.where(t > 0, prev, 0.0)

    def glu_body(i, carry):
        rows = pl.ds(pl.multiple_of(i * CONV_RC, CONV_RC), CONV_RC)
        ext_ref[pl.ds(pl.multiple_of(CONV_HALO + i * CONV_RC, CONV_HALO), CONV_RC), :] = (
            v_ref[0, rows, :].astype(F32) * jax.nn.sigmoid(g_ref[0, rows, :].astype(F32)))
        return carry

    lax.fori_loop(0, ts // CONV_RC, glu_body, 0)
    _conv_taps(ext_ref, w_ref, y_ref, width,
               lambda cols: jnp.broadcast_to(cb_ref[:, cols], (CONV_RC, cols.stop - cols.start)))
    def store(rows, z):
        o_ref[0, rows, :] = (z * jax.nn.sigmoid(z)).astype(o_ref.dtype)

    _layer_norm_loop(ts, lambda rows: y_ref[rows, :], store, ng_ref[...], nb_ref[...])


def _conv_short_kernel(gb_ref, gc_ref, h_ref, gcp_ref, hp_ref, w_ref, o_ref, ext_ref, y_ref):
    t = pl.program_id(1)
    ts = gb_ref.shape[1]
    width = w_ref.shape[0]
    prev = gcp_ref[0].astype(F32) * hp_ref[0].astype(F32)
    ext_ref[0:CONV_HALO, :] = jnp.where(t > 0, prev, 0.0)

    def mul_body(i, carry):
        rows = pl.ds(pl.multiple_of(i * CONV_RC, CONV_RC), CONV_RC)
        ext_ref[pl.ds(pl.multiple_of(CONV_HALO + i * CONV_RC, CONV_HALO), CONV_RC), :] = (
            gc_ref[0, rows, :].astype(F32) * h_ref[0, rows, :].astype(F32))
        return carry

    lax.fori_loop(0, ts // CONV_RC, mul_body, 0)
    _conv_taps(ext_ref, w_ref, y_ref, width,
               lambda cols: jnp.zeros((CONV_RC, cols.stop - cols.start), F32))

    def out_body(i, carry):
        rows = pl.ds(pl.multiple_of(i * CONV_RC, CONV_RC), CONV_RC)
        o_ref[0, rows, :] = (gb_ref[0, rows, :].astype(F32) * y_ref[rows, :]).astype(o_ref.dtype)
        return carry

    lax.fori_loop(0, ts // CONV_RC, out_body, 0)


def _halo_spec(ch, ts, col_block):
    per = ts // CONV_HALO
    return pl.BlockSpec((1, CONV_HALO, ch), lambda b, t: (b, jnp.maximum(t * per - 1, 0), col_block))


def conv_gated(u, val_blk, gate_blk, ch, conv_w, conv_b, norm_g, norm_b):
    B, S, _ = u.shape
    ts = min(CONV_ROWS, S)
    assert conv_w.shape[0] - 1 <= CONV_HALO and S % ts == 0
    cur = lambda blk: pl.BlockSpec((1, ts, ch), lambda b, t: (b, t, blk))
    vec = pl.BlockSpec((1, ch), lambda b, t: (0, 0))
    return pl.pallas_call(
        _conv_gated_kernel,
        grid=(B, S // ts),
        in_specs=[cur(val_blk), cur(gate_blk), _halo_spec(ch, ts, val_blk), _halo_spec(ch, ts, gate_blk),
                  pl.BlockSpec(conv_w.shape, lambda b, t: (0, 0)), vec, vec, vec],
        out_specs=pl.BlockSpec((1, ts, ch), lambda b, t: (b, t, 0)),
        out_shape=jax.ShapeDtypeStruct((B, S, ch), BF16),
        scratch_shapes=[pltpu.VMEM((CONV_HALO + ts, ch), F32), pltpu.VMEM((ts, ch), F32)],
        compiler_params=pltpu.CompilerParams(dimension_semantics=("parallel", "parallel"),
                                             vmem_limit_bytes=VMEM_LIMIT_BYTES),
        name="conv_gated",
    )(u, u, u, u, conv_w, conv_b.reshape(1, ch), norm_g.reshape(1, ch), norm_b.reshape(1, ch))


def conv_short(u, gb_blk, gc_blk, h_blk, ch, conv_w):
    B, S, _ = u.shape
    ts = min(CONV_ROWS, S)
    assert conv_w.shape[0] - 1 <= CONV_HALO and S % ts == 0
    cur = lambda blk: pl.BlockSpec((1, ts, ch), lambda b, t: (b, t, blk))
    return pl.pallas_call(
        _conv_short_kernel,
        grid=(B, S // ts),
        in_specs=[cur(gb_blk), cur(gc_blk), cur(h_blk), _halo_spec(ch, ts, gc_blk), _halo_spec(ch, ts, h_blk),
                  pl.BlockSpec(conv_w.shape, lambda b, t: (0, 0))],
        out_specs=pl.BlockSpec((1, ts, ch), lambda b, t: (b, t, 0)),
        out_shape=jax.ShapeDtypeStruct((B, S, ch), BF16),
        scratch_shapes=[pltpu.VMEM((CONV_HALO + ts, ch), F32), pltpu.VMEM((ts, ch), F32)],
        compiler_params=pltpu.CompilerParams(dimension_semantics=("parallel", "parallel"),
                                             vmem_limit_bytes=VMEM_LIMIT_BYTES),
        name="conv_short",
    )(u, u, u, u, u, conv_w)


def _dilated_kernel(relb_ref, q_ref, kp_ref, kc_ref, vp_ref, vc_ref, o_ref, lse_ref, bias_ref, *, dilation, steps):
    n = pl.program_id(2)
    BLK = DIL_BLOCK
    i_pos = lax.broadcasted_iota(I32, (BLK, 2 * BLK), 0)
    j_pos = lax.broadcasted_iota(I32, (BLK, 2 * BLK), 1)
    m = BLK + i_pos - j_pos

    @pl.when((pl.program_id(0) == 0) & (pl.program_id(1) == 0) & (n == 0))
    def _():
        for h in range(DIL_SLOTS):
            bias_ref[h] = _t5_bias_tile(relb_ref, h, m * dilation)

    valid = (m >= 0) & (m <= steps) & ((n > 0) | (j_pos >= BLK))
    for h in range(DIL_SLOTS):
        cols = slice(h * HEAD_DIM, (h + 1) * HEAD_DIM)
        q = q_ref[0, 0, :, cols]
        kb = jnp.concatenate([kp_ref[0, 0, :, cols], kc_ref[0, 0, :, cols]], axis=0)
        vb = jnp.concatenate([vp_ref[0, 0, :, cols], vc_ref[0, 0, :, cols]], axis=0)
        lg = _nt_dot(q, kb) * (HEAD_DIM ** -0.5) + bias_ref[h]
        lg = jnp.where(valid, lg, NEG_INF)
        mx = jnp.max(lg, axis=-1, keepdims=True)
        p = jnp.exp(lg - mx)
        l = jnp.sum(p, axis=-1, keepdims=True)
        o = jnp.dot(p.astype(BF16), vb, preferred_element_type=F32)
        o_ref[0, 0, :, cols] = o * (1.0 / l)
        lse_ref[0, 0, :, cols] = jnp.broadcast_to(mx + jnp.log(l), (BLK, HEAD_DIM))


def dilated_group(qkv, rel_bias, q_blk, k_blk, v_blk, window):
    B, dilation, L, N = qkv.shape
    W = DIL_SLOTS * HEAD_DIM
    assert L % DIL_BLOCK == 0 and N % W == 0 and window // dilation <= DIL_BLOCK
    cur = lambda blk: pl.BlockSpec((1, 1, DIL_BLOCK, W), lambda b, r, n: (b, r, n, blk))
    prev = lambda blk: pl.BlockSpec((1, 1, DIL_BLOCK, W), lambda b, r, n: (b, r, jnp.maximum(n - 1, 0), blk))
    out = pl.BlockSpec((1, 1, DIL_BLOCK, W), lambda b, r, n: (b, r, n, 0))
    kern = functools.partial(_dilated_kernel, dilation=dilation, steps=window // dilation)
    return pl.pallas_call(
        kern,
        grid=(B, dilation, L // DIL_BLOCK),
        in_specs=[pl.BlockSpec(memory_space=pltpu.SMEM), cur(q_blk), prev(k_blk), cur(k_blk), prev(v_blk), cur(v_blk)],
        out_specs=[out, out],
        out_shape=[jax.ShapeDtypeStruct((B, dilation, L, W), F32)] * 2,
        scratch_shapes=[pltpu.VMEM((DIL_SLOTS, DIL_BLOCK, 2 * DIL_BLOCK), F32)],
        compiler_params=pltpu.CompilerParams(dimension_semantics=("arbitrary", "arbitrary", "arbitrary")),
        name=f"dilated_d{dilation}",
    )(rel_bias, qkv, qkv, qkv, qkv, qkv)


def _dil_combine_kernel(*refs, dilations):
    g = len(dilations)
    o_refs, l_refs, out_ref, tmp_refs = refs[:g], refs[g:2 * g], refs[2 * g], refs[2 * g + 1:]

    def in_position_order(ref, tmp_ref, d):
        if d == 1:
            return ref[0, 0]
        rows = ref.shape[2]
        for c in range(tmp_ref.shape[0]):
            for r in range(d):
                tmp_ref[c, pl.ds(r, rows, stride=d), :] = ref[0, r, :, c * LANES:(c + 1) * LANES]
        return jnp.concatenate([tmp_ref[c] for c in range(tmp_ref.shape[0])], axis=1)

    ls = [in_position_order(l_refs[i], tmp_refs[2 * i], d) for i, d in enumerate(dilations)]
    mx = functools.reduce(jnp.maximum, ls)
    es = [jnp.exp(l - mx) for l in ls]
    den = functools.reduce(jnp.add, es)
    num = None
    for i, d in enumerate(dilations):
        term = es[i] * in_position_order(o_refs[i], tmp_refs[2 * i + 1], d)
        num = term if num is None else num + term
    out_ref[0] = (num / den).astype(out_ref.dtype)


DIL_COMBINE_ROWS = 512


def dilated_combine(outs, lses):
    B, _, _, W = outs[0].shape
    dilations = tuple(o.shape[1] for o in outs)
    S = outs[0].shape[1] * outs[0].shape[2]
    ts = min(DIL_COMBINE_ROWS, S)
    assert all(ts % (d * SUBLANES) == 0 for d in dilations)
    specs = [pl.BlockSpec((1, d, ts // d, W), lambda b, t: (b, 0, t, 0)) for d in dilations]
    return pl.pallas_call(
        functools.partial(_dil_combine_kernel, dilations=dilations),
        grid=(B, S // ts),
        in_specs=specs + specs,
        out_specs=pl.BlockSpec((1, ts, W), lambda b, t: (b, t, 0)),
        out_shape=jax.ShapeDtypeStruct((B, S, W), BF16),
        scratch_shapes=[pltpu.VMEM((W // LANES, ts, LANES), F32)] * (2 * len(dilations)),
        compiler_params=pltpu.CompilerParams(dimension_semantics=("parallel", "parallel"),
                                             vmem_limit_bytes=VMEM_LIMIT_BYTES),
        name="dilated_combine",
    )(*outs, *lses)


EVEN_ATTN_COLS = 6144
EVEN_QIDX_COL = 0
EVEN_Q_COL = IDX_HEADS * IDX_DIM
EVEN_CKV_COL = EVEN_Q_COL + DSA_HEADS * HEAD_DIM
EVEN_KIDX_COL = EVEN_CKV_COL + KV_RANK
EVEN_WIDX_COL = EVEN_KIDX_COL + IDX_DIM


def _even_col_tiles(ca):
    q0 = 2 * ca
    ckv0 = q0 + DSA_HEADS * HEAD_DIM
    qidx0 = ckv0 + KV_RANK
    kidx0 = qidx0 + IDX_HEADS * IDX_DIM
    tiles = _col_tiles((qidx0, IDX_HEADS * IDX_DIM), (q0, DSA_HEADS * HEAD_DIM), (ckv0, KV_RANK),
                       (kidx0, PROJ_COLS), (0, 2 * ca))
    assert len(tiles) * PROJ_COLS == EVEN_ATTN_COLS + 2 * ca
    return tiles


def mixer_even(xs, w_in, conv_w, conv_b, norm_g, norm_b, kv_norm_g, w_uk, w_uv, rel_bias, B, S):
    ca = conv_w.shape[-1]
    assert EVEN_ATTN_COLS % ca == 0
    w_b = w_in.astype(BF16)
    w_b = jnp.pad(w_b, ((0, 0), (0, (-w_b.shape[1]) % PROJ_COLS)))
    u = project(xs, w_b, _even_col_tiles(ca), S).reshape(B, S, -1)
    a = conv_gated(u, EVEN_ATTN_COLS // ca, EVEN_ATTN_COLS // ca + 1, ca, conv_w, conv_b, norm_g, norm_b)
    cn, ct = ckv_norm(u, kv_norm_g.reshape(1, KV_RANK), EVEN_CKV_COL // KV_RANK, min(DSA_TS, S))
    o = dsa_attention(u, cn, ct, w_uk.reshape(KV_RANK, DSA_HEADS * HEAD_DIM).astype(BF16),
                      jnp.transpose(w_uv, (1, 2, 0)).astype(BF16), rel_bias,
                      (EVEN_QIDX_COL // (IDX_HEADS * IDX_DIM), EVEN_Q_COL // (DSA_HEADS * HEAD_DIM),
                       EVEN_WIDX_COL // 128, EVEN_KIDX_COL // IDX_DIM), min(IDX_TOPK, S // 4))
    return a, o


def mixer_odd(xs, w_in, conv_w, rel_bias, B, S):
    cc = conv_w.shape[-1]
    W = DIL_SLOTS * HEAD_DIM
    assert (3 * cc) % W == 0 and DIL_CONFIGS[0][1] == 1
    w_b = w_in.astype(BF16)
    qkv_tiles = lambda g: [(3 * cc + (i * DIL_GROUPS + g) * W, W) for i in range(3)]
    u = project(xs, w_b, _col_tiles((0, 3 * cc), *qkv_tiles(0)), S).reshape(B, S, -1)
    c_out = conv_short(u, 0, 1, 2, cc, conv_w)
    q_blk = 3 * cc // W
    results = [dilated_group(u.reshape(B, 1, S, -1), rel_bias, q_blk, q_blk + 1, q_blk + 2, DIL_CONFIGS[0][0])]
    for g, (window, dilation) in list(enumerate(DIL_CONFIGS))[1:]:
        qkv = project(xs, w_b, _col_tiles(*qkv_tiles(g)), S, dilation)
        results.append(dilated_group(qkv, rel_bias, 0, 1, 2, window))
    o = dilated_combine([r[0] for r in results], [r[1] for r in results])
    return c_out, o


def kernel(x, w_ffn_in, w_ffn_out, post_ln_g, post_ln_b, rel_bias, w_in_e, w_out_e, conv_a_w, conv_a_b,
           conv_a_norm_g, conv_a_norm_b, kv_norm_g, w_uk, w_uv, w_in_o, w_out_o, conv_c_w):
    B, S, D = x.shape
    M = B * S
    depth = w_ffn_in.shape[0]
    xs = x.reshape(M, D)
    w_ffn_in_b = w_ffn_in.astype(BF16)
    w_ffn_out_b = w_ffn_out.astype(BF16)
    for layer in range(depth):
        g = post_ln_g[layer].reshape(3, 1, D)
        b = post_ln_b[layer].reshape(3, 1, D)
        xs = ffn_ln(xs, w_ffn_in_b, w_ffn_out_b, layer, 0, g[0], b[0])
        if layer % 2 == 0:
            e = layer // 2
            conv_out, attn_out = mixer_even(xs, w_in_e[e], conv_a_w[e], conv_a_b[e], conv_a_norm_g[e],
                                            conv_a_norm_b[e], kv_norm_g[e], w_uk[e], w_uv[e], rel_bias, B, S)
            w_out = w_out_e[e]
        else:
            o = layer // 2
            conv_out, attn_out = mixer_odd(xs, w_in_o[o], conv_c_w[o], rel_bias, B, S)
            w_out = w_out_o[o]
        xs = out_ln(conv_out.reshape(M, -1), attn_out.reshape(M, -1), w_out.astype(BF16), xs, g[1], b[1])
        xs = ffn_ln(xs, w_ffn_in_b, w_ffn_out_b, layer, 1, g[2], b[2])
    return xs.reshape(B, S, D)
```

```python
import functools
import math

import jax
import jax.numpy as jnp
import numpy as np
from jax import lax
from jax.experimental import pallas as pl
from jax.experimental.pallas import tpu as pltpu

DEPTH = 2
ALPHA = (2 * DEPTH) ** 0.25
LN_EPS = 1e-5
NEG_INF = -1e30

HEAD_DIM = 128
DSA_HEADS = 8
KV_RANK = 512
IDX_HEADS = 32
IDX_DIM = 128
IDX_TOPK = 256
DIL_SLOTS = 8
DIL_CONFIGS = ((128, 1), (512, 4), (2048, 16))
DIL_GROUPS = len(DIL_CONFIGS)
DIL_HEADS = DIL_SLOTS * DIL_GROUPS
DIL_BLOCK = 128
REL_BUCKETS = 32
REL_MAX_DIST = 128

V7X_VMEM_BYTES = 64 * 1024 * 1024
VMEM_LIMIT_BYTES = V7X_VMEM_BYTES - 6 * 1024 * 1024

BF16 = jnp.bfloat16
F32 = jnp.float32
I32 = jnp.int32
INT_MIN = -2 ** 31
SUBLANES = 8
LANES = 128


def _layer_norm_rows(y, g, b):
    mu = jnp.mean(y, axis=-1, keepdims=True)
    yc = y - mu
    var = jnp.mean(yc * yc, axis=-1, keepdims=True)
    return yc * lax.rsqrt(var + LN_EPS) * g + b


LN_CHUNK = 16
LN_GROUP = 4


def _layer_norm_loop(n_rows, load_fn, store_fn, g, b):
    step = LN_CHUNK * LN_GROUP
    assert n_rows % step == 0

    def body(i, carry):
        base = pl.multiple_of(i * step, step)
        rows = [pl.ds(base + c * LN_CHUNK, LN_CHUNK) for c in range(LN_GROUP)]
        ys = [load_fn(r) for r in rows]
        for r, y in zip(rows, ys):
            store_fn(r, _layer_norm_rows(y, g, b))
        return carry

    lax.fori_loop(0, n_rows // step, body, 0)


def _nt_dot(a, b):
    return lax.dot_general(a, b, (((1,), (1,)), ((), ())), preferred_element_type=F32)


FFN_ROWS = 512
FFN_HIDDEN = 256


def _ffn_ln_kernel(x_ref, wg_ref, wu_ref, wo_ref, g_ref, b_ref, o_ref, xb_ref):
    f = pl.program_id(1)

    @pl.when(f == 0)
    def _():
        xb_ref[...] = x_ref[...].astype(BF16)
        o_ref[...] = jnp.zeros_like(o_ref)

    xb = xb_ref[...]
    gate = jnp.dot(xb, wg_ref[...], preferred_element_type=F32)
    up = jnp.dot(xb, wu_ref[...], preferred_element_type=F32)
    h = (gate * jax.nn.sigmoid(gate) * up).astype(BF16)
    o_ref[...] += jnp.dot(h, wo_ref[...], preferred_element_type=F32)

    @pl.when(f == pl.num_programs(1) - 1)
    def _():
        def store(rows, z):
            o_ref[rows, :] = z

        _layer_norm_loop(x_ref.shape[0], lambda rows: ALPHA * x_ref[rows, :] + 0.5 * o_ref[rows, :], store,
                         g_ref[...], b_ref[...])


def ffn_ln(x, w_in, w_out, layer, idx, g, b):
    m, d = x.shape
    f = w_out.shape[2]
    tm = min(FFN_ROWS, m)
    tf = FFN_HIDDEN
    assert m % tm == 0 and f % tf == 0
    nf = f // tf
    return pl.pallas_call(
        _ffn_ln_kernel,
        grid=(m // tm, nf),
        in_specs=[
            pl.BlockSpec((tm, d), lambda i, j: (i, 0)),
            pl.BlockSpec((None, None, d, tf), lambda i, j: (layer, idx, 0, j)),
            pl.BlockSpec((None, None, d, tf), lambda i, j: (layer, idx, 0, j + nf)),
            pl.BlockSpec((None, None, tf, d), lambda i, j: (layer, idx, j, 0)),
            pl.BlockSpec((1, d), lambda i, j: (0, 0)),
            pl.BlockSpec((1, d), lambda i, j: (0, 0)),
        ],
        out_specs=pl.BlockSpec((tm, d), lambda i, j: (i, 0)),
        out_shape=jax.ShapeDtypeStruct((m, d), F32),
        scratch_shapes=[pltpu.VMEM((tm, d), BF16)],
        compiler_params=pltpu.CompilerParams(
            dimension_semantics=("parallel", "arbitrary"),
            vmem_limit_bytes=VMEM_LIMIT_BYTES),
        name="ffn_ln",
    )(x, w_in, w_in, w_out, g, b)


PROJ_ROWS = 1024
PROJ_COLS = 512


def _proj_kernel(tiles_ref, x_ref, w_ref, o_ref, xb_ref, *res_ref, dilation):
    del tiles_ref

    @pl.when(pl.program_id(1) == 0)
    def _():
        xb_ref[...] = x_ref[...].astype(BF16)

    y = jnp.dot(xb_ref[...], w_ref[...], preferred_element_type=F32)
    if dilation == 1:
        o_ref[...] = y.astype(o_ref.dtype)
    else:
        y_ref, = res_ref
        rows = y_ref.shape[1] // dilation
        for c in range(y_ref.shape[0]):
            lanes = slice(c * LANES, (c + 1) * LANES)
            y_ref[c] = y[:, lanes]
            for r in range(dilation):
                o_ref[0, r, :, lanes] = y_ref[c, pl.ds(r, rows, stride=dilation), :].astype(o_ref.dtype)


def _col_tiles(*pieces):
    tiles = []
    for start, width in pieces:
        assert start % PROJ_COLS == 0 and width % PROJ_COLS == 0
        tiles += list(range(start // PROJ_COLS, (start + width) // PROJ_COLS))
    return np.asarray(tiles, np.int32)


def project(x, w, col_tiles, seq_len, dilation=1):
    m, k = x.shape
    tm = min(PROJ_ROWS, seq_len)
    tn = PROJ_COLS
    n = tn * len(col_tiles)
    assert m % seq_len == 0 and seq_len % tm == 0 and w.shape[1] % tn == 0 and tm % (dilation * 16) == 0
    if dilation == 1:
        out_spec = pl.BlockSpec((tm, tn), lambda i, j, tiles: (i, j))
        out_shape = jax.ShapeDtypeStruct((m, n), BF16)
        scratch = []
    else:
        per = seq_len // tm
        out_spec = pl.BlockSpec((1, dilation, tm // dilation, tn), lambda i, j, tiles: (i // per, 0, i % per, j))
        out_shape = jax.ShapeDtypeStruct((m // seq_len, dilation, seq_len // dilation, n), BF16)
        scratch = [pltpu.VMEM((tn // LANES, tm, LANES), F32)]
    return pl.pallas_call(
        functools.partial(_proj_kernel, dilation=dilation),
        grid_spec=pltpu.PrefetchScalarGridSpec(
            num_scalar_prefetch=1,
            grid=(m // tm, len(col_tiles)),
            in_specs=[
                pl.BlockSpec((tm, k), lambda i, j, tiles: (i, 0)),
                pl.BlockSpec((k, tn), lambda i, j, tiles: (0, tiles[j])),
            ],
            out_specs=out_spec,
            scratch_shapes=[pltpu.VMEM((tm, k), BF16)] + scratch,
        ),
        out_shape=out_shape,
        compiler_params=pltpu.CompilerParams(
            dimension_semantics=("parallel", "arbitrary"),
            vmem_limit_bytes=VMEM_LIMIT_BYTES),
        name="project",
    )(jnp.asarray(col_tiles), x, w)


OUT_ROWS = 512
OUT_K = 1024


def _out_ln_kernel(a_ref, a2_ref, w_ref, x_ref, g_ref, b_ref, o_ref, *, n_first):
    kk = pl.program_id(1)

    @pl.when(kk == 0)
    def _():
        o_ref[...] = jnp.zeros_like(o_ref)

    @pl.when(kk < n_first)
    def _():
        o_ref[...] += jnp.dot(a_ref[...], w_ref[...], preferred_element_type=F32)

    @pl.when(kk >= n_first)
    def _():
        o_ref[...] += jnp.dot(a2_ref[...], w_ref[...], preferred_element_type=F32)

    @pl.when(kk == pl.num_programs(1) - 1)
    def _():
        def store(rows, z):
            o_ref[rows, :] = z

        _layer_norm_loop(x_ref.shape[0], lambda rows: ALPHA * x_ref[rows, :] + o_ref[rows, :], store,
                         g_ref[...], b_ref[...])


def out_ln(a, a2, w, x, g, b):
    m, k1 = a.shape
    k2 = a2.shape[1]
    d = w.shape[1]
    tm = min(OUT_ROWS, m)
    tk = min(OUT_K, k1, k2)
    assert m % tm == 0 and k1 % tk == 0 and k2 % tk == 0 and w.shape[0] == k1 + k2
    n1, n2 = k1 // tk, k2 // tk
    return pl.pallas_call(
        functools.partial(_out_ln_kernel, n_first=n1),
        grid=(m // tm, n1 + n2),
        in_specs=[
            pl.BlockSpec((tm, tk), lambda i, j: (i, jnp.minimum(j, n1 - 1))),
            pl.BlockSpec((tm, tk), lambda i, j: (i, jnp.maximum(j - n1, 0))),
            pl.BlockSpec((tk, d), lambda i, j: (j, 0)),
            pl.BlockSpec((tm, d), lambda i, j: (i, 0)),
            pl.BlockSpec((1, d), lambda i, j: (0, 0)),
            pl.BlockSpec((1, d), lambda i, j: (0, 0)),
        ],
        out_specs=pl.BlockSpec((tm, d), lambda i, j: (i, 0)),
        out_shape=jax.ShapeDtypeStruct((m, d), F32),
        compiler_params=pltpu.CompilerParams(
            dimension_semantics=("parallel", "arbitrary"),
            vmem_limit_bytes=VMEM_LIMIT_BYTES),
        name="out_ln",
    )(a, a2, w, x, g, b)


def _t5_thresholds():
    d = np.arange(0, 2 * REL_MAX_DIST)
    max_exact = REL_BUCKETS // 2
    nf = np.maximum(d, 1).astype(np.float32)
    large = max_exact + (np.log(nf / np.float32(max_exact)) / np.float32(math.log(REL_MAX_DIST / max_exact))
                         * np.float32(REL_BUCKETS - max_exact)).astype(np.int32)
    large = np.minimum(large, REL_BUCKETS - 1)
    bucket = np.where(d < max_exact, d, large)
    return [int(np.argmax(bucket >= k)) for k in range(REL_BUCKETS)]


T5_THR = _t5_thresholds()


def _t5_bias_tile(relb_ref, head, dist):
    acc = jnp.full(dist.shape, relb_ref[0, head], F32)
    for k in range(1, REL_BUCKETS):
        acc = jnp.where(dist >= T5_THR[k], relb_ref[k, head], acc)
    return acc


DSA_TQ = 128
DSA_TS = 512
DSA_HG = 2
DSA_BITS_ALWAYS = 20
DSA_BIT_STAGES = (DSA_BITS_ALWAYS, 23, 26, 31)
SUB = 128
assert T5_THR[REL_BUCKETS - 1] <= SUB


def _ckv_kernel(c_ref, g_ref, cn_ref, ct_ref):
    c = c_ref[0].astype(F32)
    y = c * lax.rsqrt(jnp.mean(c * c, axis=-1, keepdims=True) + LN_EPS) * g_ref[...]
    cn_ref[0] = y.astype(BF16)
    ct_ref[0, 0] = y.T.astype(BF16)


def ckv_norm(u, g, col_block, ts):
    B, S, _ = u.shape
    R = KV_RANK
    return pl.pallas_call(
        _ckv_kernel,
        grid=(B, S // ts),
        in_specs=[pl.BlockSpec((1, ts, R), lambda b, j: (b, j, col_block)),
                  pl.BlockSpec((1, R), lambda b, j: (0, 0))],
        out_specs=[pl.BlockSpec((1, ts, R), lambda b, j: (b, j, 0)),
                   pl.BlockSpec((1, 1, R, ts), lambda b, j: (b, j, 0, 0))],
        out_shape=[jax.ShapeDtypeStruct((B, S, R), BF16),
                   jax.ShapeDtypeStruct((B, S // ts, R, ts), BF16)],
        compiler_params=pltpu.CompilerParams(dimension_semantics=("parallel", "parallel")),
        name="ckv_norm",
    )(u, g)


def _dsa_kernel(relb_ref, qi_ref, q_ref, wi_ref, k_ref, c_ref, ct_ref, wuk_ref, wuvt_ref, o_ref,
                qm_ref, qlt_ref, wt_ref, key_ref, acc_ref, m_ref, l_ref, bias_ref, p_ref, a_ref, *, n_keep):
    b = pl.program_id(0)
    qt = pl.program_id(1)
    S = k_ref.shape[1]
    TS = ct_ref.shape[3]
    TQ = DSA_TQ
    H = DSA_HEADS
    n_sub = TS // SUB
    n_s = (qt * TQ + TQ + TS - 1) // TS
    idx_bits = int(S).bit_length()

    @pl.when((b == 0) & (qt == 0))
    def _():
        s_l = lax.broadcasted_iota(I32, (SUB, TQ), 0)
        t_l = lax.broadcasted_iota(I32, (SUB, TQ), 1)
        for which in range(2):
            for h in range(H):
                bias_ref[which, h] = (_t5_bias_tile(relb_ref, h, t_l - s_l + SUB * which)
                                      - relb_ref[REL_BUCKETS - 1, h])

    for h in range(IDX_HEADS):
        qm_ref[h * TQ:(h + 1) * TQ, :] = qi_ref[0, :, h * IDX_DIM:(h + 1) * IDX_DIM]
    for h in range(H):
        ql = _nt_dot(wuk_ref[:, h * HEAD_DIM:(h + 1) * HEAD_DIM], q_ref[0, :, h * HEAD_DIM:(h + 1) * HEAD_DIM])
        qlt_ref[:, h * TQ:(h + 1) * TQ] = (ql * (HEAD_DIM ** -0.5)).astype(BF16)
    wt_ref[...] = wi_ref[0].astype(F32).T * (IDX_HEADS ** -0.5 * IDX_DIM ** -0.5)

    t_pos = qt * TQ + lax.broadcasted_iota(I32, (TS, TQ), 1)
    s_loc = lax.broadcasted_iota(I32, (TS, TQ), 0)

    def idx_body(j, carry):
        row0 = pl.multiple_of(j * TS, TS)
        kt = k_ref[0, pl.ds(row0, TS), :]
        score = jnp.zeros((TS, TQ), F32)
        for hp in range(IDX_HEADS // 2):
            sc = _nt_dot(kt, qm_ref[hp * 2 * TQ:(hp + 1) * 2 * TQ, :])
            for i in range(2):
                h = 2 * hp + i
                score = score + jnp.maximum(sc[:, i * TQ:(i + 1) * TQ], 0.0) * wt_ref[h:h + 1, :]
        bits = pltpu.bitcast(score, I32)
        key = bits ^ ((bits >> 31) & 0x7FFFFFFF)
        key = jnp.where(row0 + s_loc <= t_pos, key, INT_MIN)
        key_ref[pl.ds(row0, TS), :] = key
        return carry

    lax.fori_loop(0, n_s, idx_body, 0)

    def count(pred_fn):
        def body(j, part):
            row0 = pl.multiple_of(j * TS, TS)
            blk = key_ref[pl.ds(row0, TS), :]
            hit = pred_fn(blk, row0).astype(I32)
            return part + hit.reshape(TS // 8, 8, TQ).sum(axis=0)
        part = lax.fori_loop(0, n_s, body, jnp.zeros((8, TQ), I32))
        return part.sum(axis=0, keepdims=True)

    c0 = count(lambda blk, row0: blk >= 0)
    nonneg = c0 >= n_keep
    thr = jnp.where(nonneg, jnp.zeros((1, TQ), I32), jnp.full((1, TQ), INT_MIN, I32))
    c_ge = jnp.where(nonneg, c0, n_s * TS)

    def bit_body(i, carry):
        thr, c_ge = carry
        cand = thr | (jnp.int32(1) << (30 - i))
        c = count(lambda blk, row0: blk >= cand)
        take = c >= n_keep
        return jnp.where(take, cand, thr), jnp.where(take, c, c_ge)

    def bits(lo, hi, carry):
        exact = jnp.min(jnp.where(carry[1] == n_keep, 1, 0)) == 1
        return lax.cond(exact, lambda: carry, lambda: lax.fori_loop(lo, hi, bit_body, carry))

    carry = lax.fori_loop(0, DSA_BITS_ALWAYS, bit_body, (thr, c_ge))
    for lo, hi in zip(DSA_BIT_STAGES[:-1], DSA_BIT_STAGES[1:]):
        carry = bits(lo, hi, carry)
    thr, c_ge = carry

    def tie_search():
        need = n_keep - count(lambda blk, row0: blk > thr)

        def tie_body(i, x):
            cand = x | (jnp.int32(1) << (idx_bits - 1 - i))
            c = count(lambda blk, row0: (blk == thr) & (row0 + s_loc < cand))
            return jnp.where(c < need, cand, x)
        return lax.fori_loop(0, idx_bits, tie_body, jnp.zeros((1, TQ), I32))

    tie_pos = lax.cond(jnp.max(c_ge) > n_keep, tie_search, lambda: jnp.full((1, TQ), 2 ** idx_bits - 1, I32))

    m_ref[...] = jnp.full(m_ref.shape, NEG_INF, F32)
    l_ref[...] = jnp.zeros(l_ref.shape, F32)
    acc_ref[...] = jnp.zeros(acc_ref.shape, F32)

    def apply_pending(j_prev, pend):
        ckv_t = ct_ref[0, j_prev]
        for hg in range(H // DSA_HG):
            gcols = slice(hg * DSA_HG * TQ, (hg + 1) * DSA_HG * TQ)
            pv = jnp.dot(ckv_t, p_ref[pend, :, gcols], preferred_element_type=F32)
            acc_ref[:, gcols] = acc_ref[:, gcols] * a_ref[pend, :, gcols] + pv

    def att_tile(j, near, pend):
        row0 = pl.multiple_of(j * TS, TS)
        key = key_ref[pl.ds(row0, TS), :]
        s_pos = row0 + s_loc
        sel = ((key > thr) | ((key == thr) & (s_pos <= tie_pos))) & (s_pos <= t_pos)
        ckv = c_ref[0, pl.ds(row0, TS), :]
        apply_pending(jnp.maximum(j - 1, 0), pend)
        for hg in range(H // DSA_HG):
            gcols = slice(hg * DSA_HG * TQ, (hg + 1) * DSA_HG * TQ)
            lg = jnp.dot(ckv, qlt_ref[:, gcols], preferred_element_type=F32)
            for i in range(DSA_HG):
                h = DSA_HG * hg + i
                cols = slice(h * TQ, (h + 1) * TQ)
                x = lg[:, i * TQ:(i + 1) * TQ]
                if near:
                    pieces = []
                    for sb in range(n_sub):
                        delta = qt - (j * n_sub + sb)
                        pieces.append(jnp.where(delta == 0, bias_ref[0, h],
                                                jnp.where(delta == 1, bias_ref[1, h], 0.0)))
                    x = x + (pieces[0] if n_sub == 1 else jnp.concatenate(pieces, axis=0))
                lh = jnp.where(sel, x, NEG_INF)
                m_old = m_ref[:, cols]
                m_new = jnp.maximum(m_old, jnp.max(lh, axis=0, keepdims=True))
                p = jnp.exp(lh - m_new)
                alpha = jnp.exp(m_old - m_new)
                l_ref[:, cols] = alpha * l_ref[:, cols] + jnp.sum(p, axis=0, keepdims=True)
                m_ref[:, cols] = m_new
                p_ref[1 - pend, :, cols] = p.astype(BF16)
                a_ref[1 - pend, :, cols] = alpha

    p_ref[1] = jnp.zeros(p_ref.shape[1:], BF16)
    a_ref[1] = jnp.ones(a_ref.shape[1:], F32)

    def pair_body(near):
        def body(i, carry):
            att_tile(2 * i, near, pend=1)
            att_tile(2 * i + 1, near, pend=0)
            return carry
        return body

    n_far_pairs = (jnp.maximum(qt - 1, 0) // n_sub) // 2
    n_pairs = n_s // 2
    lax.fori_loop(0, n_far_pairs, pair_body(False), 0)
    lax.fori_loop(n_far_pairs, n_pairs, pair_body(True), 0)

    @pl.when(n_s % 2 == 1)
    def _():
        att_tile(n_s - 1, True, pend=1)
        apply_pending(n_s - 1, 0)

    @pl.when(n_s % 2 == 0)
    def _():
        apply_pending(n_s - 1, 1)

    for h in range(H):
        cols = slice(h * TQ, (h + 1) * TQ)
        ol_t = (acc_ref[:, cols] * (1.0 / l_ref[:, cols])).astype(BF16)
        o_t = jnp.dot(wuvt_ref[h], ol_t, preferred_element_type=F32)
        o_ref[0, :, h * HEAD_DIM:(h + 1) * HEAD_DIM] = o_t.T.astype(o_ref.dtype)


def dsa_attention(u, ckv_n, ckv_t, w_uk2, w_uvt, rel_bias, cols, n_keep):
    B, S, _ = u.shape
    ts = ckv_t.shape[3]
    H, R = DSA_HEADS, KV_RANK
    qi_blk, q_blk, wi_blk, k_blk = cols
    kern = functools.partial(_dsa_kernel, n_keep=n_keep)
    return pl.pallas_call(
        kern,
        grid=(B, S // DSA_TQ),
        in_specs=[
            pl.BlockSpec(memory_space=pltpu.SMEM),
            pl.BlockSpec((1, DSA_TQ, IDX_HEADS * IDX_DIM), lambda b, t: (b, t, qi_blk)),
            pl.BlockSpec((1, DSA_TQ, H * HEAD_DIM), lambda b, t: (b, t, q_blk)),
            pl.BlockSpec((1, DSA_TQ, 128), lambda b, t: (b, t, wi_blk)),
            pl.BlockSpec((1, S, IDX_DIM), lambda b, t: (b, 0, k_blk)),
            pl.BlockSpec((1, S, R), lambda b, t: (b, 0, 0)),
            pl.BlockSpec((1, S // ts, R, ts), lambda b, t: (b, 0, 0, 0)),
            pl.BlockSpec((R, H * HEAD_DIM), lambda b, t: (0, 0)),
            pl.BlockSpec((H, HEAD_DIM, R), lambda b, t: (0, 0, 0)),
        ],
        out_specs=pl.BlockSpec((1, DSA_TQ, H * HEAD_DIM), lambda b, t: (b, t, 0)),
        out_shape=jax.ShapeDtypeStruct((B, S, H * HEAD_DIM), BF16),
        scratch_shapes=[
            pltpu.VMEM((IDX_HEADS * DSA_TQ, IDX_DIM), BF16),
            pltpu.VMEM((R, H * DSA_TQ), BF16),
            pltpu.VMEM((128, DSA_TQ), F32),
            pltpu.VMEM((S, DSA_TQ), I32),
            pltpu.VMEM((R, H * DSA_TQ), F32),
            pltpu.VMEM((1, H * DSA_TQ), F32),
            pltpu.VMEM((1, H * DSA_TQ), F32),
            pltpu.VMEM((2, H, SUB, DSA_TQ), F32),
            pltpu.VMEM((2, ts, H * DSA_TQ), BF16),
            pltpu.VMEM((2, 1, H * DSA_TQ), F32),
        ],
        compiler_params=pltpu.CompilerParams(
            dimension_semantics=("arbitrary", "arbitrary"),
            vmem_limit_bytes=VMEM_LIMIT_BYTES),
        name="dsa_attention",
    )(rel_bias, u, u, u, u, ckv_n, ckv_t, w_uk2, w_uvt)


CONV_ROWS = 256
CONV_HALO = 32
CONV_RC = 64
CONV_CW = 512


def _conv_taps(ext_ref, w_ref, y_ref, width, init_fn):
    ts, ch = y_ref.shape
    cw = min(CONV_CW, ch)
    n = CONV_RC + SUBLANES
    assert CONV_HALO >= SUBLANES * ((width - 1) // SUBLANES + 1)

    def body(i, carry):
        r0 = pl.multiple_of(i * CONV_RC, CONV_RC)
        for cc in range(ch // cw):
            cols = slice(cc * cw, (cc + 1) * cw)
            acc = init_fn(cols)
            for b in range(min(SUBLANES, width)):
                z = None
                for a in range((width - b + SUBLANES - 1) // SUBLANES):
                    j = SUBLANES * a + b
                    win = ext_ref[pl.ds(r0 + (CONV_HALO - SUBLANES * (a + 1)), n), cols]
                    term = win * w_ref[width - 1 - j:width - j, cols]
                    z = term if z is None else z + term
                if b == 0:
                    acc = acc + z[SUBLANES:, :]
                else:
                    acc = acc + pltpu.roll(z, n - (SUBLANES - b), axis=0)[:CONV_RC, :]
            y_ref[pl.ds(r0, CONV_RC), cols] = acc
        return carry

    lax.fori_loop(0, ts // CONV_RC, body, 0)


def _conv_gated_kernel(v_ref, g_ref, vp_ref, gp_ref, w_ref, cb_ref, ng_ref, nb_ref, o_ref, ext_ref, y_ref):
    t = pl.program_id(1)
    ts = v_ref.shape[1]
    width = w_ref.shape[0]
    prev = vp_ref[0].astype(F32) * jax.nn.sigmoid(gp_ref[0].astype(F32))
    ext_ref[0:CONV_HALO, :] = jnp.where(t > 0, prev, 0.0)

    def glu_body(i, carry):
        rows = pl.ds(pl.multiple_of(i * CONV_RC, CONV_RC), CONV_RC)
        ext_ref[pl.ds(pl.multiple_of(CONV_HALO + i * CONV_RC, CONV_HALO), CONV_RC), :] = (
            v_ref[0, rows, :].astype(F32) * jax.nn.sigmoid(g_ref[0, rows, :].astype(F32)))
        return carry

    lax.fori_loop(0, ts // CONV_RC, glu_body, 0)
    _conv_taps(ext_ref, w_ref, y_ref, width,
               lambda cols: jnp.broadcast_to(cb_ref[:, cols], (CONV_RC, cols.stop - cols.start)))
    def store(rows, z):
        o_ref[0, rows, :] = (z * jax.nn.sigmoid(z)).astype(o_ref.dtype)

    _layer_norm_loop(ts, lambda rows: y_ref[rows, :], store, ng_ref[...], nb_ref[...])


def _conv_short_kernel(gb_ref, gc_ref, h_ref, gcp_ref, hp_ref, w_ref, o_ref, ext_ref, y_ref):
    t = pl.program_id(1)
    ts = gb_ref.shape[1]
    width = w_ref.shape[0]
    prev = gcp_ref[0].astype(F32) * hp_ref[0].astype(F32)
    ext_ref[0:CONV_HALO, :] = jnp.where(t > 0, prev, 0.0)

    def mul_body(i, carry):
        rows = pl.ds(pl.multiple_of(i * CONV_RC, CONV_RC), CONV_RC)
        ext_ref[pl.ds(pl.multiple_of(CONV_HALO + i * CONV_RC, CONV_HALO), CONV_RC), :] = (
            gc_ref[0, rows, :].astype(F32) * h_ref[0, rows, :].astype(F32))
        return carry

    lax.fori_loop(0, ts // CONV_RC, mul_body, 0)
    _conv_taps(ext_ref, w_ref, y_ref, width,
               lambda cols: jnp.zeros((CONV_RC, cols.stop - cols.start), F32))

    def out_body(i, carry):
        rows = pl.ds(pl.multiple_of(i * CONV_RC, CONV_RC), CONV_RC)
        o_ref[0, rows, :] = (gb_ref[0, rows, :].astype(F32) * y_ref[rows, :]).astype(o_ref.dtype)
        return carry

    lax.fori_loop(0, ts // CONV_RC, out_body, 0)


def _halo_spec(ch, ts, col_block):
    per = ts // CONV_HALO
    return pl.BlockSpec((1, CONV_HALO, ch), lambda b, t: (b, jnp.maximum(t * per - 1, 0), col_block))


def conv_gated(u, val_blk, gate_blk, ch, conv_w, conv_b, norm_g, norm_b):
    B, S, _ = u.shape
    ts = min(CONV_ROWS, S)
    assert conv_w.shape[0] - 1 <= CONV_HALO and S % ts == 0
    cur = lambda blk: pl.BlockSpec((1, ts, ch), lambda b, t: (b, t, blk))
    vec = pl.BlockSpec((1, ch), lambda b, t: (0, 0))
    return pl.pallas_call(
        _conv_gated_kernel,
        grid=(B, S // ts),
        in_specs=[cur(val_blk), cur(gate_blk), _halo_spec(ch, ts, val_blk), _halo_spec(ch, ts, gate_blk),
                  pl.BlockSpec(conv_w.shape, lambda b, t: (0, 0)), vec, vec, vec],
        out_specs=pl.BlockSpec((1, ts, ch), lambda b, t: (b, t, 0)),
        out_shape=jax.ShapeDtypeStruct((B, S, ch), BF16),
        scratch_shapes=[pltpu.VMEM((CONV_HALO + ts, ch), F32), pltpu.VMEM((ts, ch), F32)],
        compiler_params=pltpu.CompilerParams(dimension_semantics=("parallel", "parallel"),
                                             vmem_limit_bytes=VMEM_LIMIT_BYTES),
        name="conv_gated",
    )(u, u, u, u, conv_w, conv_b.reshape(1, ch), norm_g.reshape(1, ch), norm_b.reshape(1, ch))


def conv_short(u, gb_blk, gc_blk, h_blk, ch, conv_w):
    B, S, _ = u.shape
    ts = min(CONV_ROWS, S)
    assert conv_w.shape[0] - 1 <= CONV_HALO and S % ts == 0
    cur = lambda blk: pl.BlockSpec((1, ts, ch), lambda b, t: (b, t, blk))
    return pl.pallas_call(
        _conv_short_kernel,
        grid=(B, S // ts),
        in_specs=[cur(gb_blk), cur(gc_blk), cur(h_blk), _halo_spec(ch, ts, gc_blk), _halo_spec(ch, ts, h_blk),
                  pl.BlockSpec(conv_w.shape, lambda b, t: (0, 0))],
        out_specs=pl.BlockSpec((1, ts, ch), lambda b, t: (b, t, 0)),
        out_shape=jax.ShapeDtypeStruct((B, S, ch), BF16),
        scratch_shapes=[pltpu.VMEM((CONV_HALO + ts, ch), F32), pltpu.VMEM((ts, ch), F32)],
        compiler_params=pltpu.CompilerParams(dimension_semantics=("parallel", "parallel"),
                                             vmem_limit_bytes=VMEM_LIMIT_BYTES),
        name="conv_short",
    )(u, u, u, u, u, conv_w)


def _dilated_kernel(relb_ref, q_ref, kp_ref, kc_ref, vp_ref, vc_ref, o_ref, lse_ref, bias_ref, *, dilation, steps):
    n = pl.program_id(2)
    BLK = DIL_BLOCK
    i_pos = lax.broadcasted_iota(I32, (BLK, 2 * BLK), 0)
    j_pos = lax.broadcasted_iota(I32, (BLK, 2 * BLK), 1)
    m = BLK + i_pos - j_pos

    @pl.when((pl.program_id(0) == 0) & (pl.program_id(1) == 0) & (n == 0))
    def _():
        for h in range(DIL_SLOTS):
            bias_ref[h] = _t5_bias_tile(relb_ref, h, m * dilation)

    valid = (m >= 0) & (m <= steps) & ((n > 0) | (j_pos >= BLK))
    for h in range(DIL_SLOTS):
        cols = slice(h * HEAD_DIM, (h + 1) * HEAD_DIM)
        q = q_ref[0, 0, :, cols]
        kb = jnp.concatenate([kp_ref[0, 0, :, cols], kc_ref[0, 0, :, cols]], axis=0)
        vb = jnp.concatenate([vp_ref[0, 0, :, cols], vc_ref[0, 0, :, cols]], axis=0)
        lg = _nt_dot(q, kb) * (HEAD_DIM ** -0.5) + bias_ref[h]
        lg = jnp.where(valid, lg, NEG_INF)
        mx = jnp.max(lg, axis=-1, keepdims=True)
        p = jnp.exp(lg - mx)
        l = jnp.sum(p, axis=-1, keepdims=True)
        o = jnp.dot(p.astype(BF16), vb, preferred_element_type=F32)
        o_ref[0, 0, :, cols] = o * (1.0 / l)
        lse_ref[0, 0, :, cols] = jnp.broadcast_to(mx + jnp.log(l), (BLK, HEAD_DIM))


def dilated_group(qkv, rel_bias, q_blk, k_blk, v_blk, window):
    B, dilation, L, N = qkv.shape
    W = DIL_SLOTS * HEAD_DIM
    assert L % DIL_BLOCK == 0 and N % W == 0 and window // dilation <= DIL_BLOCK
    cur = lambda blk: pl.BlockSpec((1, 1, DIL_BLOCK, W), lambda b, r, n: (b, r, n, blk))
    prev = lambda blk: pl.BlockSpec((1, 1, DIL_BLOCK, W), lambda b, r, n: (b, r, jnp.maximum(n - 1, 0), blk))
    out = pl.BlockSpec((1, 1, DIL_BLOCK, W), lambda b, r, n: (b, r, n, 0))
    kern = functools.partial(_dilated_kernel, dilation=dilation, steps=window // dilation)
    return pl.pallas_call(
        kern,
        grid=(B, dilation, L // DIL_BLOCK),
        in_specs=[pl.BlockSpec(memory_space=pltpu.SMEM), cur(q_blk), prev(k_blk), cur(k_blk), prev(v_blk), cur(v_blk)],
        out_specs=[out, out],
        out_shape=[jax.ShapeDtypeStruct((B, dilation, L, W), F32)] * 2,
        scratch_shapes=[pltpu.VMEM((DIL_SLOTS, DIL_BLOCK, 2 * DIL_BLOCK), F32)],
        compiler_params=pltpu.CompilerParams(dimension_semantics=("arbitrary", "arbitrary", "arbitrary")),
        name=f"dilated_d{dilation}",
    )(rel_bias, qkv, qkv, qkv, qkv, qkv)


def _dil_combine_kernel(*refs, dilations):
    g = len(dilations)
    o_refs, l_refs, out_ref, tmp_refs = refs[:g], refs[g:2 * g], refs[2 * g], refs[2 * g + 1:]

    def in_position_order(ref, tmp_ref, d):
        if d == 1:
            return ref[0, 0]
        rows = ref.shape[2]
        for c in range(tmp_ref.shape[0]):
            for r in range(d):
                tmp_ref[c, pl.ds(r, rows, stride=d), :] = ref[0, r, :, c * LANES:(c + 1) * LANES]
        return jnp.concatenate([tmp_ref[c] for c in range(tmp_ref.shape[0])], axis=1)

    ls = [in_position_order(l_refs[i], tmp_refs[2 * i], d) for i, d in enumerate(dilations)]
    mx = functools.reduce(jnp.maximum, ls)
    es = [jnp.exp(l - mx) for l in ls]
    den = functools.reduce(jnp.add, es)
    num = None
    for i, d in enumerate(dilations):
        term = es[i] * in_position_order(o_refs[i], tmp_refs[2 * i + 1], d)
        num = term if num is None else num + term
    out_ref[0] = (num / den).astype(out_ref.dtype)


DIL_COMBINE_ROWS = 512


def dilated_combine(outs, lses):
    B, _, _, W = outs[0].shape
    dilations = tuple(o.shape[1] for o in outs)
    S = outs[0].shape[1] * outs[0].shape[2]
    ts = min(DIL_COMBINE_ROWS, S)
    assert all(ts % (d * SUBLANES) == 0 for d in dilations)
    specs = [pl.BlockSpec((1, d, ts // d, W), lambda b, t: (b, 0, t, 0)) for d in dilations]
    return pl.pallas_call(
        functools.partial(_dil_combine_kernel, dilations=dilations),
        grid=(B, S // ts),
        in_specs=specs + specs,
        out_specs=pl.BlockSpec((1, ts, W), lambda b, t: (b, t, 0)),
        out_shape=jax.ShapeDtypeStruct((B, S, W), BF16),
        scratch_shapes=[pltpu.VMEM((W // LANES, ts, LANES), F32)] * (2 * len(dilations)),
        compiler_params=pltpu.CompilerParams(dimension_semantics=("parallel", "parallel"),
                                             vmem_limit_bytes=VMEM_LIMIT_BYTES),
        name="dilated_combine",
    )(*outs, *lses)


EVEN_ATTN_COLS = 6144
EVEN_QIDX_COL = 0
EVEN_Q_COL = IDX_HEADS * IDX_DIM
EVEN_CKV_COL = EVEN_Q_COL + DSA_HEADS * HEAD_DIM
EVEN_KIDX_COL = EVEN_CKV_COL + KV_RANK
EVEN_WIDX_COL = EVEN_KIDX_COL + IDX_DIM


def _even_col_tiles(ca):
    q0 = 2 * ca
    ckv0 = q0 + DSA_HEADS * HEAD_DIM
    qidx0 = ckv0 + KV_RANK
    kidx0 = qidx0 + IDX_HEADS * IDX_DIM
    tiles = _col_tiles((qidx0, IDX_HEADS * IDX_DIM), (q0, DSA_HEADS * HEAD_DIM), (ckv0, KV_RANK),
                       (kidx0, PROJ_COLS), (0, 2 * ca))
    assert len(tiles) * PROJ_COLS == EVEN_ATTN_COLS + 2 * ca
    return tiles


def mixer_even(xs, w_in, conv_w, conv_b, norm_g, norm_b, kv_norm_g, w_uk, w_uv, rel_bias, B, S):
    ca = conv_w.shape[-1]
    assert EVEN_ATTN_COLS % ca == 0
    w_b = w_in.astype(BF16)
    w_b = jnp.pad(w_b, ((0, 0), (0, (-w_b.shape[1]) % PROJ_COLS)))
    u = project(xs, w_b, _even_col_tiles(ca), S).reshape(B, S, -1)
    a = conv_gated(u, EVEN_ATTN_COLS // ca, EVEN_ATTN_COLS // ca + 1, ca, conv_w, conv_b, norm_g, norm_b)
    cn, ct = ckv_norm(u, kv_norm_g.reshape(1, KV_RANK), EVEN_CKV_COL // KV_RANK, min(DSA_TS, S))
    o = dsa_attention(u, cn, ct, w_uk.reshape(KV_RANK, DSA_HEADS * HEAD_DIM).astype(BF16),
                      jnp.transpose(w_uv, (1, 2, 0)).astype(BF16), rel_bias,
                      (EVEN_QIDX_COL // (IDX_HEADS * IDX_DIM), EVEN_Q_COL // (DSA_HEADS * HEAD_DIM),
                       EVEN_WIDX_COL // 128, EVEN_KIDX_COL // IDX_DIM), min(IDX_TOPK, S // 4))
    return a, o


def mixer_odd(xs, w_in, conv_w, rel_bias, B, S):
    cc = conv_w.shape[-1]
    W = DIL_SLOTS * HEAD_DIM
    assert (3 * cc) % W == 0 and DIL_CONFIGS[0][1] == 1
    w_b = w_in.astype(BF16)
    qkv_tiles = lambda g: [(3 * cc + (i * DIL_GROUPS + g) * W, W) for i in range(3)]
    u = project(xs, w_b, _col_tiles((0, 3 * cc), *qkv_tiles(0)), S).reshape(B, S, -1)
    c_out = conv_short(u, 0, 1, 2, cc, conv_w)
    q_blk = 3 * cc // W
    results = [dilated_group(u.reshape(B, 1, S, -1), rel_bias, q_blk, q_blk + 1, q_blk + 2, DIL_CONFIGS[0][0])]
    for g, (window, dilation) in list(enumerate(DIL_CONFIGS))[1:]:
        qkv = project(xs, w_b, _col_tiles(*qkv_tiles(g)), S, dilation)
        results.append(dilated_group(qkv, rel_bias, 0, 1, 2, window))
    o = dilated_combine([r[0] for r in results], [r[1] for r in results])
    return c_out, o


def kernel(x, w_ffn_in, w_ffn_out, post_ln_g, post_ln_b, rel_bias, w_in_e, w_out_e, conv_a_w, conv_a_b,
           conv_a_norm_g, conv_a_norm_b, kv_norm_g, w_uk, w_uv, w_in_o, w_out_o, conv_c_w):
    B, S, D = x.shape
    M = B * S
    depth = w_ffn_in.shape[0]
    xs = x.reshape(M, D)
    w_ffn_in_b = w_ffn_in.astype(BF16)
    w_ffn_out_b = w_ffn_out.astype(BF16)
    for layer in range(depth):
        g = post_ln_g[layer].reshape(3, 1, D)
        b = post_ln_b[layer].reshape(3, 1, D)
        xs = ffn_ln(xs, w_ffn_in_b, w_ffn_out_b, layer, 0, g[0], b[0])
        if layer % 2 == 0:
            e = layer // 2
            conv_out, attn_out = mixer_even(xs, w_in_e[e], conv_a_w[e], conv_a_b[e], conv_a_norm_g[e],
                                            conv_a_norm_b[e], kv_norm_g[e], w_uk[e], w_uv[e], rel_bias, B, S)
            w_out = w_out_e[e]
        else:
            o = layer // 2
            conv_out, attn_out = mixer_odd(xs, w_in_o[o], conv_c_w[o], rel_bias, B, S)
            w_out = w_out_o[o]
        xs = out_ln(conv_out.reshape(M, -1), attn_out.reshape(M, -1), w_out.astype(BF16), xs, g[1], b[1])
        xs = ffn_ln(xs, w_ffn_in_b, w_ffn_out_b, layer, 1, g[2], b[2])
    return xs.reshape(B, S, D)
```

```python
import functools
import math

import jax
import jax.numpy as jnp
import numpy as np
from jax import lax
from jax.experimental import pallas as pl
from jax.experimental.pallas import tpu as pltpu

DEPTH = 2
ALPHA = (2 * DEPTH) ** 0.25
LN_EPS = 1e-5
NEG_INF = -1e30

HEAD_DIM = 128
DSA_HEADS = 8
KV_RANK = 512
IDX_HEADS = 32
IDX_DIM = 128
IDX_TOPK = 256
DIL_SLOTS = 8
DIL_CONFIGS = ((128, 1), (512, 4), (2048, 16))
DIL_GROUPS = len(DIL_CONFIGS)
DIL_HEADS = DIL_SLOTS * DIL_GROUPS
DIL_BLOCK = 128
REL_BUCKETS = 32
REL_MAX_DIST = 128

V7X_VMEM_BYTES = 64 * 1024 * 1024
VMEM_LIMIT_BYTES = V7X_VMEM_BYTES - 6 * 1024 * 1024

BF16 = jnp.bfloat16
F32 = jnp.float32
I32 = jnp.int32
INT_MIN = -2 ** 31
SUBLANES = 8
LANES = 128


def _layer_norm_rows(y, g, b):
    mu = jnp.mean(y, axis=-1, keepdims=True)
    yc = y - mu
    var = jnp.mean(yc * yc, axis=-1, keepdims=True)
    return yc * lax.rsqrt(var + LN_EPS) * g + b


LN_CHUNK = 16
LN_GROUP = 4


def _layer_norm_loop(n_rows, load_fn, store_fn, g, b):
    step = LN_CHUNK * LN_GROUP
    assert n_rows % step == 0

    def body(i, carry):
        base = pl.multiple_of(i * step, step)
        rows = [pl.ds(base + c * LN_CHUNK, LN_CHUNK) for c in range(LN_GROUP)]
        ys = [load_fn(r) for r in rows]
        for r, y in zip(rows, ys):
            store_fn(r, _layer_norm_rows(y, g, b))
        return carry

    lax.fori_loop(0, n_rows // step, body, 0)


def _nt_dot(a, b):
    return lax.dot_general(a, b, (((1,), (1,)), ((), ())), preferred_element_type=F32)


FFN_ROWS = 512
FFN_HIDDEN = 256


def _ffn_ln_kernel(x_ref, wg_ref, wu_ref, wo_ref, g_ref, b_ref, o_ref, xb_ref):
    f = pl.program_id(1)

    @pl.when(f == 0)
    def _():
        xb_ref[...] = x_ref[...].astype(BF16)
        o_ref[...] = jnp.zeros_like(o_ref)

    xb = xb_ref[...]
    gate = jnp.dot(xb, wg_ref[...], preferred_element_type=F32)
    up = jnp.dot(xb, wu_ref[...], preferred_element_type=F32)
    h = (gate * jax.nn.sigmoid(gate) * up).astype(BF16)
    o_ref[...] += jnp.dot(h, wo_ref[...], preferred_element_type=F32)

    @pl.when(f == pl.num_programs(1) - 1)
    def _():
        def store(rows, z):
            o_ref[rows, :] = z

        _layer_norm_loop(x_ref.shape[0], lambda rows: ALPHA * x_ref[rows, :] + 0.5 * o_ref[rows, :], store,
                         g_ref[...], b_ref[...])


def ffn_ln(x, w_in, w_out, layer, idx, g, b):
    m, d = x.shape
    f = w_out.shape[2]
    tm = min(FFN_ROWS, m)
    tf = FFN_HIDDEN
    assert m % tm == 0 and f % tf == 0
    nf = f // tf
    return pl.pallas_call(
        _ffn_ln_kernel,
        grid=(m // tm, nf),
        in_specs=[
            pl.BlockSpec((tm, d), lambda i, j: (i, 0)),
            pl.BlockSpec((None, None, d, tf), lambda i, j: (layer, idx, 0, j)),
            pl.BlockSpec((None, None, d, tf), lambda i, j: (layer, idx, 0, j + nf)),
            pl.BlockSpec((None, None, tf, d), lambda i, j: (layer, idx, j, 0)),
            pl.BlockSpec((1, d), lambda i, j: (0, 0)),
            pl.BlockSpec((1, d), lambda i, j: (0, 0)),
        ],
        out_specs=pl.BlockSpec((tm, d), lambda i, j: (i, 0)),
        out_shape=jax.ShapeDtypeStruct((m, d), F32),
        scratch_shapes=[pltpu.VMEM((tm, d), BF16)],
        compiler_params=pltpu.CompilerParams(
            dimension_semantics=("parallel", "arbitrary"),
            vmem_limit_bytes=VMEM_LIMIT_BYTES),
        name="ffn_ln",
    )(x, w_in, w_in, w_out, g, b)


PROJ_ROWS = 1024
PROJ_COLS = 512


def _proj_kernel(tiles_ref, x_ref, w_ref, o_ref, xb_ref, *res_ref, dilation):
    del tiles_ref

    @pl.when(pl.program_id(1) == 0)
    def _():
        xb_ref[...] = x_ref[...].astype(BF16)

    y = jnp.dot(xb_ref[...], w_ref[...], preferred_element_type=F32)
    if dilation == 1:
        o_ref[...] = y.astype(o_ref.dtype)
    else:
        y_ref, = res_ref
        rows = y_ref.shape[1] // dilation
        for c in range(y_ref.shape[0]):
            lanes = slice(c * LANES, (c + 1) * LANES)
            y_ref[c] = y[:, lanes]
            for r in range(dilation):
                o_ref[0, r, :, lanes] = y_ref[c, pl.ds(r, rows, stride=dilation), :].astype(o_ref.dtype)


def _col_tiles(*pieces):
    tiles = []
    for start, width in pieces:
        assert start % PROJ_COLS == 0 and width % PROJ_COLS == 0
        tiles += list(range(start // PROJ_COLS, (start + width) // PROJ_COLS))
    return np.asarray(tiles, np.int32)


def project(x, w, col_tiles, seq_len, dilation=1):
    m, k = x.shape
    tm = min(PROJ_ROWS, seq_len)
    tn = PROJ_COLS
    n = tn * len(col_tiles)
    assert m % seq_len == 0 and seq_len % tm == 0 and w.shape[1] % tn == 0 and tm % (dilation * 16) == 0
    if dilation == 1:
        out_spec = pl.BlockSpec((tm, tn), lambda i, j, tiles: (i, j))
        out_shape = jax.ShapeDtypeStruct((m, n), BF16)
        scratch = []
    else:
        per = seq_len // tm
        out_spec = pl.BlockSpec((1, dilation, tm // dilation, tn), lambda i, j, tiles: (i // per, 0, i % per, j))
        out_shape = jax.ShapeDtypeStruct((m // seq_len, dilation, seq_len // dilation, n), BF16)
        scratch = [pltpu.VMEM((tn // LANES, tm, LANES), F32)]
    return pl.pallas_call(
        functools.partial(_proj_kernel, dilation=dilation),
        grid_spec=pltpu.PrefetchScalarGridSpec(
            num_scalar_prefetch=1,
            grid=(m // tm, len(col_tiles)),
            in_specs=[
                pl.BlockSpec((tm, k), lambda i, j, tiles: (i, 0)),
                pl.BlockSpec((k, tn), lambda i, j, tiles: (0, tiles[j])),
            ],
            out_specs=out_spec,
            scratch_shapes=[pltpu.VMEM((tm, k), BF16)] + scratch,
        ),
        out_shape=out_shape,
        compiler_params=pltpu.CompilerParams(
            dimension_semantics=("parallel", "arbitrary"),
            vmem_limit_bytes=VMEM_LIMIT_BYTES),
        name="project",
    )(jnp.asarray(col_tiles), x, w)


OUT_ROWS = 512
OUT_K = 1024


def _out_ln_kernel(a_ref, a2_ref, w_ref, x_ref, g_ref, b_ref, o_ref, *, n_first):
    kk = pl.program_id(1)

    @pl.when(kk == 0)
    def _():
        o_ref[...] = jnp.zeros_like(o_ref)

    @pl.when(kk < n_first)
    def _():
        o_ref[...] += jnp.dot(a_ref[...], w_ref[...], preferred_element_type=F32)

    @pl.when(kk >= n_first)
    def _():
        o_ref[...] += jnp.dot(a2_ref[...], w_ref[...], preferred_element_type=F32)

    @pl.when(kk == pl.num_programs(1) - 1)
    def _():
        def store(rows, z):
            o_ref[rows, :] = z

        _layer_norm_loop(x_ref.shape[0], lambda rows: ALPHA * x_ref[rows, :] + o_ref[rows, :], store,
                         g_ref[...], b_ref[...])


def out_ln(a, a2, w, x, g, b):
    m, k1 = a.shape
    k2 = a2.shape[1]
    d = w.shape[1]
    tm = min(OUT_ROWS, m)
    tk = min(OUT_K, k1, k2)
    assert m % tm == 0 and k1 % tk == 0 and k2 % tk == 0 and w.shape[0] == k1 + k2
    n1, n2 = k1 // tk, k2 // tk
    return pl.pallas_call(
        functools.partial(_out_ln_kernel, n_first=n1),
        grid=(m // tm, n1 + n2),
        in_specs=[
            pl.BlockSpec((tm, tk), lambda i, j: (i, jnp.minimum(j, n1 - 1))),
            pl.BlockSpec((tm, tk), lambda i, j: (i, jnp.maximum(j - n1, 0))),
            pl.BlockSpec((tk, d), lambda i, j: (j, 0)),
            pl.BlockSpec((tm, d), lambda i, j: (i, 0)),
            pl.BlockSpec((1, d), lambda i, j: (0, 0)),
            pl.BlockSpec((1, d), lambda i, j: (0, 0)),
        ],
        out_specs=pl.BlockSpec((tm, d), lambda i, j: (i, 0)),
        out_shape=jax.ShapeDtypeStruct((m, d), F32),
        compiler_params=pltpu.CompilerParams(
            dimension_semantics=("parallel", "arbitrary"),
            vmem_limit_bytes=VMEM_LIMIT_BYTES),
        name="out_ln",
    )(a, a2, w, x, g, b)


def _t5_thresholds():
    d = np.arange(0, 2 * REL_MAX_DIST)
    max_exact = REL_BUCKETS // 2
    nf = np.maximum(d, 1).astype(np.float32)
    large = max_exact + (np.log(nf / np.float32(max_exact)) / np.float32(math.log(REL_MAX_DIST / max_exact))
                         * np.float32(REL_BUCKETS - max_exact)).astype(np.int32)
    large = np.minimum(large, REL_BUCKETS - 1)
    bucket = np.where(d < max_exact, d, large)
    return [int(np.argmax(bucket >= k)) for k in range(REL_BUCKETS)]


T5_THR = _t5_thresholds()


def _t5_bias_tile(relb_ref, head, dist):
    acc = jnp.full(dist.shape, relb_ref[0, head], F32)
    for k in range(1, REL_BUCKETS):
        acc = jnp.where(dist >= T5_THR[k], relb_ref[k, head], acc)
    return acc


DSA_TQ = 128
DSA_TS = 512
DSA_HG = 2
DSA_BITS_ALWAYS = 20
DSA_BIT_STAGES = (DSA_BITS_ALWAYS, 22, 24, 26, 28, 31)
SUB = 128
assert T5_THR[REL_BUCKETS - 1] <= SUB


def _ckv_kernel(c_ref, g_ref, cn_ref, ct_ref):
    c = c_ref[0].astype(F32)
    y = c * lax.rsqrt(jnp.mean(c * c, axis=-1, keepdims=True) + LN_EPS) * g_ref[...]
    cn_ref[0] = y.astype(BF16)
    ct_ref[0, 0] = y.T.astype(BF16)


def ckv_norm(u, g, col_block, ts):
    B, S, _ = u.shape
    R = KV_RANK
    return pl.pallas_call(
        _ckv_kernel,
        grid=(B, S // ts),
        in_specs=[pl.BlockSpec((1, ts, R), lambda b, j: (b, j, col_block)),
                  pl.BlockSpec((1, R), lambda b, j: (0, 0))],
        out_specs=[pl.BlockSpec((1, ts, R), lambda b, j: (b, j, 0)),
                   pl.BlockSpec((1, 1, R, ts), lambda b, j: (b, j, 0, 0))],
        out_shape=[jax.ShapeDtypeStruct((B, S, R), BF16),
                   jax.ShapeDtypeStruct((B, S // ts, R, ts), BF16)],
        compiler_params=pltpu.CompilerParams(dimension_semantics=("parallel", "parallel")),
        name="ckv_norm",
    )(u, g)


def _dsa_kernel(relb_ref, qi_ref, q_ref, wi_ref, k_ref, c_ref, ct_ref, wuk_ref, wuvt_ref, o_ref,
                qm_ref, qlt_ref, wt_ref, key_ref, acc_ref, m_ref, l_ref, bias_ref, p_ref, a_ref, *, n_keep):
    b = pl.program_id(0)
    qt = pl.program_id(1)
    S = k_ref.shape[1]
    TS = ct_ref.shape[3]
    TQ = DSA_TQ
    H = DSA_HEADS
    n_sub = TS // SUB
    n_s = (qt * TQ + TQ + TS - 1) // TS
    idx_bits = int(S).bit_length()

    @pl.when((b == 0) & (qt == 0))
    def _():
        s_l = lax.broadcasted_iota(I32, (SUB, TQ), 0)
        t_l = lax.broadcasted_iota(I32, (SUB, TQ), 1)
        for which in range(2):
            for h in range(H):
                bias_ref[which, h] = (_t5_bias_tile(relb_ref, h, t_l - s_l + SUB * which)
                                      - relb_ref[REL_BUCKETS - 1, h])

    for h in range(IDX_HEADS):
        qm_ref[h * TQ:(h + 1) * TQ, :] = qi_ref[0, :, h * IDX_DIM:(h + 1) * IDX_DIM]
    for h in range(H):
        ql = _nt_dot(wuk_ref[:, h * HEAD_DIM:(h + 1) * HEAD_DIM], q_ref[0, :, h * HEAD_DIM:(h + 1) * HEAD_DIM])
        qlt_ref[:, h * TQ:(h + 1) * TQ] = (ql * (HEAD_DIM ** -0.5)).astype(BF16)
    wt_ref[...] = wi_ref[0].astype(F32).T * (IDX_HEADS ** -0.5 * IDX_DIM ** -0.5)

    t_pos = qt * TQ + lax.broadcasted_iota(I32, (TS, TQ), 1)
    s_loc = lax.broadcasted_iota(I32, (TS, TQ), 0)

    def idx_body(j, carry):
        row0 = pl.multiple_of(j * TS, TS)
        kt = k_ref[0, pl.ds(row0, TS), :]
        score = jnp.zeros((TS, TQ), F32)
        for hp in range(IDX_HEADS // 2):
            sc = _nt_dot(kt, qm_ref[hp * 2 * TQ:(hp + 1) * 2 * TQ, :])
            for i in range(2):
                h = 2 * hp + i
                score = score + jnp.maximum(sc[:, i * TQ:(i + 1) * TQ], 0.0) * wt_ref[h:h + 1, :]
        bits = pltpu.bitcast(score, I32)
        key = bits ^ ((bits >> 31) & 0x7FFFFFFF)
        key = jnp.where(row0 + s_loc <= t_pos, key, INT_MIN)
        key_ref[pl.ds(row0, TS), :] = key
        return carry

    lax.fori_loop(0, n_s, idx_body, 0)

    def count(pred_fn):
        def body(j, part):
            row0 = pl.multiple_of(j * TS, TS)
            blk = key_ref[pl.ds(row0, TS), :]
            hit = pred_fn(blk, row0).astype(I32)
            return part + hit.reshape(TS // 8, 8, TQ).sum(axis=0)
        part = lax.fori_loop(0, n_s, body, jnp.zeros((8, TQ), I32))
        return part.sum(axis=0, keepdims=True)

    c0 = count(lambda blk, row0: blk >= 0)
    nonneg = c0 >= n_keep
    thr = jnp.where(nonneg, jnp.zeros((1, TQ), I32), jnp.full((1, TQ), INT_MIN, I32))
    c_ge = jnp.where(nonneg, c0, n_s * TS)

    def bit_body(i, carry):
        thr, c_ge = carry
        cand = thr | (jnp.int32(1) << (30 - i))
        c = count(lambda blk, row0: blk >= cand)
        take = c >= n_keep
        return jnp.where(take, cand, thr), jnp.where(take, c, c_ge)

    def bits(lo, hi, carry):
        exact = jnp.min(jnp.where(carry[1] == n_keep, 1, 0)) == 1
        return lax.cond(exact, lambda: carry, lambda: lax.fori_loop(lo, hi, bit_body, carry))

    carry = lax.fori_loop(0, DSA_BITS_ALWAYS, bit_body, (thr, c_ge))
    for lo, hi in zip(DSA_BIT_STAGES[:-1], DSA_BIT_STAGES[1:]):
        carry = bits(lo, hi, carry)
    thr, c_ge = carry

    def tie_search():
        need = n_keep - count(lambda blk, row0: blk > thr)

        def tie_body(i, x):
            cand = x | (jnp.int32(1) << (idx_bits - 1 - i))
            c = count(lambda blk, row0: (blk == thr) & (row0 + s_loc < cand))
            return jnp.where(c < need, cand, x)
        return lax.fori_loop(0, idx_bits, tie_body, jnp.zeros((1, TQ), I32))

    tie_pos = lax.cond(jnp.max(c_ge) > n_keep, tie_search, lambda: jnp.full((1, TQ), 2 ** idx_bits - 1, I32))

    m_ref[...] = jnp.full(m_ref.shape, NEG_INF, F32)
    l_ref[...] = jnp.zeros(l_ref.shape, F32)
    acc_ref[...] = jnp.zeros(acc_ref.shape, F32)

    def apply_pending(j_prev, pend):
        ckv_t = ct_ref[0, j_prev]
        for hg in range(H // DSA_HG):
            gcols = slice(hg * DSA_HG * TQ, (hg + 1) * DSA_HG * TQ)
            pv = jnp.dot(ckv_t, p_ref[pend, :, gcols], preferred_element_type=F32)
            acc_ref[:, gcols] = acc_ref[:, gcols] * a_ref[pend, :, gcols] + pv

    def att_tile(j, near, pend):
        row0 = pl.multiple_of(j * TS, TS)
        key = key_ref[pl.ds(row0, TS), :]
        s_pos = row0 + s_loc
        sel = ((key > thr) | ((key == thr) & (s_pos <= tie_pos))) & (s_pos <= t_pos)
        ckv = c_ref[0, pl.ds(row0, TS), :]
        apply_pending(jnp.maximum(j - 1, 0), pend)
        for hg in range(H // DSA_HG):
            gcols = slice(hg * DSA_HG * TQ, (hg + 1) * DSA_HG * TQ)
            lg = jnp.dot(ckv, qlt_ref[:, gcols], preferred_element_type=F32)
            for i in range(DSA_HG):
                h = DSA_HG * hg + i
                cols = slice(h * TQ, (h + 1) * TQ)
                x = lg[:, i * TQ:(i + 1) * TQ]
                if near:
                    pieces = []
                    for sb in range(n_sub):
                        delta = qt - (j * n_sub + sb)
                        pieces.append(jnp.where(delta == 0, bias_ref[0, h],
                                                jnp.where(delta == 1, bias_ref[1, h], 0.0)))
                    x = x + (pieces[0] if n_sub == 1 else jnp.concatenate(pieces, axis=0))
                lh = jnp.where(sel, x, NEG_INF)
                m_old = m_ref[:, cols]
                m_new = jnp.maximum(m_old, jnp.max(lh, axis=0, keepdims=True))
                p = jnp.exp(lh - m_new)
                alpha = jnp.exp(m_old - m_new)
                l_ref[:, cols] = alpha * l_ref[:, cols] + jnp.sum(p, axis=0, keepdims=True)
                m_ref[:, cols] = m_new
                p_ref[1 - pend, :, cols] = p.astype(BF16)
                a_ref[1 - pend, :, cols] = alpha

    p_ref[1] = jnp.zeros(p_ref.shape[1:], BF16)
    a_ref[1] = jnp.ones(a_ref.shape[1:], F32)

    def pair_body(near):
        def body(i, carry):
            att_tile(2 * i, near, pend=1)
            att_tile(2 * i + 1, near, pend=0)
            return carry
        return body

    n_far_pairs = (jnp.maximum(qt - 1, 0) // n_sub) // 2
    n_pairs = n_s // 2
    lax.fori_loop(0, n_far_pairs, pair_body(False), 0)
    lax.fori_loop(n_far_pairs, n_pairs, pair_body(True), 0)

    @pl.when(n_s % 2 == 1)
    def _():
        att_tile(n_s - 1, True, pend=1)
        apply_pending(n_s - 1, 0)

    @pl.when(n_s % 2 == 0)
    def _():
        apply_pending(n_s - 1, 1)

    for h in range(H):
        cols = slice(h * TQ, (h + 1) * TQ)
        ol_t = (acc_ref[:, cols] * (1.0 / l_ref[:, cols])).astype(BF16)
        o_t = jnp.dot(wuvt_ref[h], ol_t, preferred_element_type=F32)
        o_ref[0, :, h * HEAD_DIM:(h + 1) * HEAD_DIM] = o_t.T.astype(o_ref.dtype)


def dsa_attention(u, ckv_n, ckv_t, w_uk2, w_uvt, rel_bias, cols, n_keep):
    B, S, _ = u.shape
    ts = ckv_t.shape[3]
    H, R = DSA_HEADS, KV_RANK
    qi_blk, q_blk, wi_blk, k_blk = cols
    kern = functools.partial(_dsa_kernel, n_keep=n_keep)
    return pl.pallas_call(
        kern,
        grid=(B, S // DSA_TQ),
        in_specs=[
            pl.BlockSpec(memory_space=pltpu.SMEM),
            pl.BlockSpec((1, DSA_TQ, IDX_HEADS * IDX_DIM), lambda b, t: (b, t, qi_blk)),
            pl.BlockSpec((1, DSA_TQ, H * HEAD_DIM), lambda b, t: (b, t, q_blk)),
            pl.BlockSpec((1, DSA_TQ, LANES), lambda b, t: (b, t, wi_blk)),
            pl.BlockSpec((1, S, IDX_DIM), lambda b, t: (b, 0, k_blk)),
            pl.BlockSpec((1, S, R), lambda b, t: (b, 0, 0)),
            pl.BlockSpec((1, S // ts, R, ts), lambda b, t: (b, 0, 0, 0)),
            pl.BlockSpec((R, H * HEAD_DIM), lambda b, t: (0, 0)),
            pl.BlockSpec((H, HEAD_DIM, R), lambda b, t: (0, 0, 0)),
        ],
        out_specs=pl.BlockSpec((1, DSA_TQ, H * HEAD_DIM), lambda b, t: (b, t, 0)),
        out_shape=jax.ShapeDtypeStruct((B, S, H * HEAD_DIM), BF16),
        scratch_shapes=[
            pltpu.VMEM((IDX_HEADS * DSA_TQ, IDX_DIM), BF16),
            pltpu.VMEM((R, H * DSA_TQ), BF16),
            pltpu.VMEM((LANES, DSA_TQ), F32),
            pltpu.VMEM((S, DSA_TQ), I32),
            pltpu.VMEM((R, H * DSA_TQ), F32),
            pltpu.VMEM((1, H * DSA_TQ), F32),
            pltpu.VMEM((1, H * DSA_TQ), F32),
            pltpu.VMEM((2, H, SUB, DSA_TQ), F32),
            pltpu.VMEM((2, ts, H * DSA_TQ), BF16),
            pltpu.VMEM((2, 1, H * DSA_TQ), F32),
        ],
        compiler_params=pltpu.CompilerParams(
            dimension_semantics=("arbitrary", "arbitrary"),
            vmem_limit_bytes=VMEM_LIMIT_BYTES),
        name="dsa_attention",
    )(rel_bias, u, u, u, u, ckv_n, ckv_t, w_uk2, w_uvt)


CONV_ROWS = 256
CONV_HALO = 32
CONV_RC = 64
CONV_CW = 512


def _conv_taps(ext_ref, w_ref, y_ref, width, init_fn):
    ts, ch = y_ref.shape
    cw = min(CONV_CW, ch)
    n = CONV_RC + SUBLANES
    assert CONV_HALO >= SUBLANES * ((width - 1) // SUBLANES + 1)

    def body(i, carry):
        r0 = pl.multiple_of(i * CONV_RC, CONV_RC)
        for cc in range(ch // cw):
            cols = slice(cc * cw, (cc + 1) * cw)
            acc = init_fn(cols)
            for b in range(min(SUBLANES, width)):
                z = None
                for a in range((width - b + SUBLANES - 1) // SUBLANES):
                    j = SUBLANES * a + b
                    win = ext_ref[pl.ds(r0 + (CONV_HALO - SUBLANES * (a + 1)), n), cols]
                    term = win * w_ref[width - 1 - j:width - j, cols]
                    z = term if z is None else z + term
                if b == 0:
                    acc = acc + z[SUBLANES:, :]
                else:
                    acc = acc + pltpu.roll(z, n - (SUBLANES - b), axis=0)[:CONV_RC, :]
            y_ref[pl.ds(r0, CONV_RC), cols] = acc
        return carry

    lax.fori_loop(0, ts // CONV_RC, body, 0)


def _conv_gated_kernel(v_ref, g_ref, vp_ref, gp_ref, w_ref, cb_ref, ng_ref, nb_ref, o_ref, ext_ref, y_ref):
    t = pl.program_id(1)
    ts = v_ref.shape[1]
    width = w_ref.shape[0]
    prev = vp_ref[0].astype(F32) * jax.nn.sigmoid(gp_ref[0].astype(F32))
    ext_ref[0:CONV_HALO, :] = jnp.where(t > 0, prev, 0.0)

    def glu_body(i, carry):
        rows = pl.ds(pl.multiple_of(i * CONV_RC, CONV_RC), CONV_RC)
        ext_ref[pl.ds(pl.multiple_of(CONV_HALO + i * CONV_RC, CONV_HALO), CONV_RC), :] = (
            v_ref[0, rows, :].astype(F32) * jax.nn.sigmoid(g_ref[0, rows, :].astype(F32)))
        return carry

    lax.fori_loop(0, ts // CONV_RC, glu_body, 0)
    _conv_taps(ext_ref, w_ref, y_ref, width,
               lambda cols: jnp.broadcast_to(cb_ref[:, cols], (CONV_RC, cols.stop - cols.start)))
    def store(rows, z):
        o_ref[0, rows, :] = (z * jax.nn.sigmoid(z)).astype(o_ref.dtype)

    _layer_norm_loop(ts, lambda rows: y_ref[rows, :], store, ng_ref[...], nb_ref[...])


def _conv_short_kernel(gb_ref, gc_ref, h_ref, gcp_ref, hp_ref, w_ref, o_ref, ext_ref, y_ref):
    t = pl.program_id(1)
    ts = gb_ref.shape[1]
    width = w_ref.shape[0]
    prev = gcp_ref[0].astype(F32) * hp_ref[0].astype(F32)
    ext_ref[0:CONV_HALO, :] = jnp.where(t > 0, prev, 0.0)

    def mul_body(i, carry):
        rows = pl.ds(pl.multiple_of(i * CONV_RC, CONV_RC), CONV_RC)
        ext_ref[pl.ds(pl.multiple_of(CONV_HALO + i * CONV_RC, CONV_HALO), CONV_RC), :] = (
            gc_ref[0, rows, :].astype(F32) * h_ref[0, rows, :].astype(F32))
        return carry

    lax.fori_loop(0, ts // CONV_RC, mul_body, 0)
    _conv_taps(ext_ref, w_ref, y_ref, width,
               lambda cols: jnp.zeros((CONV_RC, cols.stop - cols.start), F32))

    def out_body(i, carry):
        rows = pl.ds(pl.multiple_of(i * CONV_RC, CONV_RC), CONV_RC)
        o_ref[0, rows, :] = (gb_ref[0, rows, :].astype(F32) * y_ref[rows, :]).astype(o_ref.dtype)
        return carry

    lax.fori_loop(0, ts // CONV_RC, out_body, 0)


def _halo_spec(ch, ts, col_block):
    per = ts // CONV_HALO
    return pl.BlockSpec((1, CONV_HALO, ch), lambda b, t: (b, jnp.maximum(t * per - 1, 0), col_block))


def conv_gated(u, val_blk, gate_blk, ch, conv_w, conv_b, norm_g, norm_b):
    B, S, _ = u.shape
    ts = min(CONV_ROWS, S)
    assert conv_w.shape[0] - 1 <= CONV_HALO and S % ts == 0
    cur = lambda blk: pl.BlockSpec((1, ts, ch), lambda b, t: (b, t, blk))
    vec = pl.BlockSpec((1, ch), lambda b, t: (0, 0))
    return pl.pallas_call(
        _conv_gated_kernel,
        grid=(B, S // ts),
        in_specs=[cur(val_blk), cur(gate_blk), _halo_spec(ch, ts, val_blk), _halo_spec(ch, ts, gate_blk),
                  pl.BlockSpec(conv_w.shape, lambda b, t: (0, 0)), vec, vec, vec],
        out_specs=pl.BlockSpec((1, ts, ch), lambda b, t: (b, t, 0)),
        out_shape=jax.ShapeDtypeStruct((B, S, ch), BF16),
        scratch_shapes=[pltpu.VMEM((CONV_HALO + ts, ch), F32), pltpu.VMEM((ts, ch), F32)],
        compiler_params=pltpu.CompilerParams(dimension_semantics=("parallel", "parallel"),
                                             vmem_limit_bytes=VMEM_LIMIT_BYTES),
        name="conv_gated",
    )(u, u, u, u, conv_w, conv_b.reshape(1, ch), norm_g.reshape(1, ch), norm_b.reshape(1, ch))


def conv_short(u, gb_blk, gc_blk, h_blk, ch, conv_w):
    B, S, _ = u.shape
    ts = min(CONV_ROWS, S)
    assert conv_w.shape[0] - 1 <= CONV_HALO and S % ts == 0
    cur = lambda blk: pl.BlockSpec((1, ts, ch), lambda b, t: (b, t, blk))
    return pl.pallas_call(
        _conv_short_kernel,
        grid=(B, S // ts),
        in_specs=[cur(gb_blk), cur(gc_blk), cur(h_blk), _halo_spec(ch, ts, gc_blk), _halo_spec(ch, ts, h_blk),
                  pl.BlockSpec(conv_w.shape, lambda b, t: (0, 0))],
        out_specs=pl.BlockSpec((1, ts, ch), lambda b, t: (b, t, 0)),
        out_shape=jax.ShapeDtypeStruct((B, S, ch), BF16),
        scratch_shapes=[pltpu.VMEM((CONV_HALO + ts, ch), F32), pltpu.VMEM((ts, ch), F32)],
        compiler_params=pltpu.CompilerParams(dimension_semantics=("parallel", "parallel"),
                                             vmem_limit_bytes=VMEM_LIMIT_BYTES),
        name="conv_short",
    )(u, u, u, u, u, conv_w)


def _dilated_kernel(relb_ref, q_ref, kp_ref, kc_ref, vp_ref, vc_ref, o_ref, lse_ref, bias_ref, *, dilation, steps):
    n = pl.program_id(2)
    BLK = DIL_BLOCK
    i_pos = lax.broadcasted_iota(I32, (BLK, 2 * BLK), 0)
    j_pos = lax.broadcasted_iota(I32, (BLK, 2 * BLK), 1)
    m = BLK + i_pos - j_pos

    @pl.when((pl.program_id(0) == 0) & (pl.program_id(1) == 0) & (n == 0))
    def _():
        for h in range(DIL_SLOTS):
            bias_ref[h] = _t5_bias_tile(relb_ref, h, m * dilation)

    valid = (m >= 0) & (m <= steps) & ((n > 0) | (j_pos >= BLK))
    for h in range(DIL_SLOTS):
        cols = slice(h * HEAD_DIM, (h + 1) * HEAD_DIM)
        q = q_ref[0, 0, :, cols]
        kb = jnp.concatenate([kp_ref[0, 0, :, cols], kc_ref[0, 0, :, cols]], axis=0)
        vb = jnp.concatenate([vp_ref[0, 0, :, cols], vc_ref[0, 0, :, cols]], axis=0)
        lg = _nt_dot(q, kb) * (HEAD_DIM ** -0.5) + bias_ref[h]
        lg = jnp.where(valid, lg, NEG_INF)
        mx = jnp.max(lg, axis=-1, keepdims=True)
        p = jnp.exp(lg - mx)
        l = jnp.sum(p, axis=-1, keepdims=True)
        o = jnp.dot(p.astype(BF16), vb, preferred_element_type=F32)
        o_ref[0, 0, :, cols] = o * (1.0 / l)
        lse_ref[0, 0, :, cols] = jnp.broadcast_to(mx + jnp.log(l), (BLK, HEAD_DIM))


def dilated_group(qkv, rel_bias, q_blk, k_blk, v_blk, window):
    B, dilation, L, N = qkv.shape
    W = DIL_SLOTS * HEAD_DIM
    assert L % DIL_BLOCK == 0 and N % W == 0 and window // dilation <= DIL_BLOCK
    cur = lambda blk: pl.BlockSpec((1, 1, DIL_BLOCK, W), lambda b, r, n: (b, r, n, blk))
    prev = lambda blk: pl.BlockSpec((1, 1, DIL_BLOCK, W), lambda b, r, n: (b, r, jnp.maximum(n - 1, 0), blk))
    out = pl.BlockSpec((1, 1, DIL_BLOCK, W), lambda b, r, n: (b, r, n, 0))
    kern = functools.partial(_dilated_kernel, dilation=dilation, steps=window // dilation)
    return pl.pallas_call(
        kern,
        grid=(B, dilation, L // DIL_BLOCK),
        in_specs=[pl.BlockSpec(memory_space=pltpu.SMEM), cur(q_blk), prev(k_blk), cur(k_blk), prev(v_blk), cur(v_blk)],
        out_specs=[out, out],
        out_shape=[jax.ShapeDtypeStruct((B, dilation, L, W), F32)] * 2,
        scratch_shapes=[pltpu.VMEM((DIL_SLOTS, DIL_BLOCK, 2 * DIL_BLOCK), F32)],
        compiler_params=pltpu.CompilerParams(dimension_semantics=("arbitrary", "arbitrary", "arbitrary")),
        name=f"dilated_d{dilation}",
    )(rel_bias, qkv, qkv, qkv, qkv, qkv)


def _dil_combine_kernel(*refs, dilations):
    g = len(dilations)
    o_refs, l_refs, out_ref, tmp_refs = refs[:g], refs[g:2 * g], refs[2 * g], refs[2 * g + 1:]

    def in_position_order(ref, tmp_ref, d):
        if d == 1:
            return ref[0, 0]
        rows = ref.shape[2]
        for c in range(tmp_ref.shape[0]):
            for r in range(d):
                tmp_ref[c, pl.ds(r, rows, stride=d), :] = ref[0, r, :, c * LANES:(c + 1) * LANES]
        return jnp.concatenate([tmp_ref[c] for c in range(tmp_ref.shape[0])], axis=1)

    ls = [in_position_order(l_refs[i], tmp_refs[2 * i], d) for i, d in enumerate(dilations)]
    mx = functools.reduce(jnp.maximum, ls)
    es = [jnp.exp(l - mx) for l in ls]
    den = functools.reduce(jnp.add, es)
    num = None
    for i, d in enumerate(dilations):
        term = es[i] * in_position_order(o_refs[i], tmp_refs[2 * i + 1], d)
        num = term if num is None else num + term
    out_ref[0] = (num / den).astype(out_ref.dtype)


DIL_COMBINE_ROWS = 512


def dilated_combine(outs, lses):
    B, _, _, W = outs[0].shape
    dilations = tuple(o.shape[1] for o in outs)
    S = outs[0].shape[1] * outs[0].shape[2]
    ts = min(DIL_COMBINE_ROWS, S)
    assert all(ts % (d * SUBLANES) == 0 for d in dilations)
    specs = [pl.BlockSpec((1, d, ts // d, W), lambda b, t: (b, 0, t, 0)) for d in dilations]
    return pl.pallas_call(
        functools.partial(_dil_combine_kernel, dilations=dilations),
        grid=(B, S // ts),
        in_specs=specs + specs,
        out_specs=pl.BlockSpec((1, ts, W), lambda b, t: (b, t, 0)),
        out_shape=jax.ShapeDtypeStruct((B, S, W), BF16),
        scratch_shapes=[pltpu.VMEM((W // LANES, ts, LANES), F32)] * (2 * len(dilations)),
        compiler_params=pltpu.CompilerParams(dimension_semantics=("parallel", "parallel"),
                                             vmem_limit_bytes=VMEM_LIMIT_BYTES),
        name="dilated_combine",
    )(*outs, *lses)


EVEN_ATTN_COLS = 6144
EVEN_QIDX_COL = 0
EVEN_Q_COL = IDX_HEADS * IDX_DIM
EVEN_CKV_COL = EVEN_Q_COL + DSA_HEADS * HEAD_DIM
EVEN_KIDX_COL = EVEN_CKV_COL + KV_RANK
EVEN_WIDX_COL = EVEN_KIDX_COL + IDX_DIM


def _even_col_tiles(ca):
    q0 = 2 * ca
    ckv0 = q0 + DSA_HEADS * HEAD_DIM
    qidx0 = ckv0 + KV_RANK
    kidx0 = qidx0 + IDX_HEADS * IDX_DIM
    tiles = _col_tiles((qidx0, IDX_HEADS * IDX_DIM), (q0, DSA_HEADS * HEAD_DIM), (ckv0, KV_RANK),
                       (kidx0, PROJ_COLS), (0, 2 * ca))
    assert len(tiles) * PROJ_COLS == EVEN_ATTN_COLS + 2 * ca
    return tiles


def mixer_even(xs, w_in, conv_w, conv_b, norm_g, norm_b, kv_norm_g, w_uk, w_uv, rel_bias, B, S):
    ca = conv_w.shape[-1]
    assert EVEN_ATTN_COLS % ca == 0
    w_b = w_in.astype(BF16)
    w_b = jnp.pad(w_b, ((0, 0), (0, (-w_b.shape[1]) % PROJ_COLS)))
    u = project(xs, w_b, _even_col_tiles(ca), S).reshape(B, S, -1)
    a = conv_gated(u, EVEN_ATTN_COLS // ca, EVEN_ATTN_COLS // ca + 1, ca, conv_w, conv_b, norm_g, norm_b)
    cn, ct = ckv_norm(u, kv_norm_g.reshape(1, KV_RANK), EVEN_CKV_COL // KV_RANK, min(DSA_TS, S))
    o = dsa_attention(u, cn, ct, w_uk.reshape(KV_RANK, DSA_HEADS * HEAD_DIM).astype(BF16),
                      jnp.transpose(w_uv, (1, 2, 0)).astype(BF16), rel_bias,
                      (EVEN_QIDX_COL // (IDX_HEADS * IDX_DIM), EVEN_Q_COL // (DSA_HEADS * HEAD_DIM),
                       EVEN_WIDX_COL // LANES, EVEN_KIDX_COL // IDX_DIM), min(IDX_TOPK, S // 4))
    return a, o


def mixer_odd(xs, w_in, conv_w, rel_bias, B, S):
    cc = conv_w.shape[-1]
    W = DIL_SLOTS * HEAD_DIM
    assert (3 * cc) % W == 0 and DIL_CONFIGS[0][1] == 1
    w_b = w_in.astype(BF16)
    qkv_tiles = lambda g: [(3 * cc + (i * DIL_GROUPS + g) * W, W) for i in range(3)]
    u = project(xs, w_b, _col_tiles((0, 3 * cc), *qkv_tiles(0)), S).reshape(B, S, -1)
    c_out = conv_short(u, 0, 1, 2, cc, conv_w)
    q_blk = 3 * cc // W
    results = [dilated_group(u.reshape(B, 1, S, -1), rel_bias, q_blk, q_blk + 1, q_blk + 2, DIL_CONFIGS[0][0])]
    for g, (window, dilation) in list(enumerate(DIL_CONFIGS))[1:]:
        qkv = project(xs, w_b, _col_tiles(*qkv_tiles(g)), S, dilation)
        results.append(dilated_group(qkv, rel_bias, 0, 1, 2, window))
    o = dilated_combine([r[0] for r in results], [r[1] for r in results])
    return c_out, o


def kernel(x, w_ffn_in, w_ffn_out, post_ln_g, post_ln_b, rel_bias, w_in_e, w_out_e, conv_a_w, conv_a_b,
           conv_a_norm_g, conv_a_norm_b, kv_norm_g, w_uk, w_uv, w_in_o, w_out_o, conv_c_w):
    B, S, D = x.shape
    M = B * S
    depth = w_ffn_in.shape[0]
    assert depth == DEPTH
    xs = x.reshape(M, D)
    w_ffn_in_b = w_ffn_in.astype(BF16)
    w_ffn_out_b = w_ffn_out.astype(BF16)
    for layer in range(depth):
        g = post_ln_g[layer].reshape(3, 1, D)
        b = post_ln_b[layer].reshape(3, 1, D)
        xs = ffn_ln(xs, w_ffn_in_b, w_ffn_out_b, layer, 0, g[0], b[0])
        if layer % 2 == 0:
            e = layer // 2
            conv_out, attn_out = mixer_even(xs, w_in_e[e], conv_a_w[e], conv_a_b[e], conv_a_norm_g[e],
                                            conv_a_norm_b[e], kv_norm_g[e], w_uk[e], w_uv[e], rel_bias, B, S)
            w_out = w_out_e[e]
        else:
            o = layer // 2
            conv_out, attn_out = mixer_odd(xs, w_in_o[o], conv_c_w[o], rel_bias, B, S)
            w_out = w_out_o[o]
        xs = out_ln(conv_out.reshape(M, -1), attn_out.reshape(M, -1), w_out.astype(BF16), xs, g[1], b[1])
        xs = ffn_ln(xs, w_ffn_in_b, w_ffn_out_b, layer, 1, g[2], b[2])
    return xs.reshape(B, S, D)
```

```python
import functools
import math

import jax
import jax.numpy as jnp
import numpy as np
from jax import lax
from jax.experimental import pallas as pl
from jax.experimental.pallas import tpu as pltpu

DEPTH = 2
ALPHA = (2 * DEPTH) ** 0.25
LN_EPS = 1e-5
NEG_INF = -1e30

HEAD_DIM = 128
DSA_HEADS = 8
KV_RANK = 512
IDX_HEADS = 32
IDX_DIM = 128
IDX_TOPK = 256
DIL_SLOTS = 8
DIL_CONFIGS = ((128, 1), (512, 4), (2048, 16))
DIL_GROUPS = len(DIL_CONFIGS)
DIL_HEADS = DIL_SLOTS * DIL_GROUPS
DIL_BLOCK = 128
REL_BUCKETS = 32
REL_MAX_DIST = 128

V7X_VMEM_BYTES = 64 * 1024 * 1024
VMEM_LIMIT_BYTES = V7X_VMEM_BYTES - 6 * 1024 * 1024

BF16 = jnp.bfloat16
F32 = jnp.float32
I32 = jnp.int32
INT_MIN = -2 ** 31
SUBLANES = 8
LANES = 128


def _layer_norm_rows(y, g, b):
    mu = jnp.mean(y, axis=-1, keepdims=True)
    yc = y - mu
    var = jnp.mean(yc * yc, axis=-1, keepdims=True)
    return yc * lax.rsqrt(var + LN_EPS) * g + b


LN_CHUNK = 16
LN_GROUP = 4


def _layer_norm_loop(n_rows, load_fn, store_fn, g, b):
    step = LN_CHUNK * LN_GROUP
    assert n_rows % step == 0

    def body(i, carry):
        base = pl.multiple_of(i * step, step)
        rows = [pl.ds(base + c * LN_CHUNK, LN_CHUNK) for c in range(LN_GROUP)]
        ys = [load_fn(r) for r in rows]
        for r, y in zip(rows, ys):
            store_fn(r, _layer_norm_rows(y, g, b))
        return carry

    lax.fori_loop(0, n_rows // step, body, 0)


def _nt_dot(a, b):
    return lax.dot_general(a, b, (((1,), (1,)), ((), ())), preferred_element_type=F32)


FFN_ROWS = 512
FFN_HIDDEN = 256


def _ffn_ln_kernel(x_ref, wg_ref, wu_ref, wo_ref, g_ref, b_ref, o_ref, xb_ref):
    f = pl.program_id(1)

    def hidden_tile_contribution():
        xb = xb_ref[...]
        gate = jnp.dot(xb, wg_ref[...], preferred_element_type=F32)
        up = jnp.dot(xb, wu_ref[...], preferred_element_type=F32)
        h = (gate * jax.nn.sigmoid(gate) * up).astype(BF16)
        return jnp.dot(h, wo_ref[...], preferred_element_type=F32)

    @pl.when(f == 0)
    def _():
        xb_ref[...] = x_ref[...].astype(BF16)
        o_ref[...] = hidden_tile_contribution()

    @pl.when(f > 0)
    def _():
        o_ref[...] += hidden_tile_contribution()

    @pl.when(f == pl.num_programs(1) - 1)
    def _():
        def store(rows, z):
            o_ref[rows, :] = z

        _layer_norm_loop(x_ref.shape[0], lambda rows: ALPHA * x_ref[rows, :] + 0.5 * o_ref[rows, :], store,
                         g_ref[...], b_ref[...])


def ffn_ln(x, w_in, w_out, layer, idx, g, b):
    m, d = x.shape
    f = w_out.shape[2]
    tm = min(FFN_ROWS, m)
    tf = FFN_HIDDEN
    assert m % tm == 0 and f % tf == 0
    nf = f // tf
    return pl.pallas_call(
        _ffn_ln_kernel,
        grid=(m // tm, nf),
        in_specs=[
            pl.BlockSpec((tm, d), lambda i, j: (i, 0)),
            pl.BlockSpec((None, None, d, tf), lambda i, j: (layer, idx, 0, j)),
            pl.BlockSpec((None, None, d, tf), lambda i, j: (layer, idx, 0, j + nf)),
            pl.BlockSpec((None, None, tf, d), lambda i, j: (layer, idx, j, 0)),
            pl.BlockSpec((1, d), lambda i, j: (0, 0)),
            pl.BlockSpec((1, d), lambda i, j: (0, 0)),
        ],
        out_specs=pl.BlockSpec((tm, d), lambda i, j: (i, 0)),
        out_shape=jax.ShapeDtypeStruct((m, d), F32),
        scratch_shapes=[pltpu.VMEM((tm, d), BF16)],
        compiler_params=pltpu.CompilerParams(
            dimension_semantics=("parallel", "arbitrary"),
            vmem_limit_bytes=VMEM_LIMIT_BYTES),
        name="ffn_ln",
    )(x, w_in, w_in, w_out, g, b)


PROJ_ROWS = 1024
PROJ_COLS = 512


def _proj_kernel(tiles_ref, x_ref, w_ref, o_ref, xb_ref, *res_ref, dilation):
    del tiles_ref

    @pl.when(pl.program_id(1) == 0)
    def _():
        xb_ref[...] = x_ref[...].astype(BF16)

    y = jnp.dot(xb_ref[...], w_ref[...], preferred_element_type=F32)
    if dilation == 1:
        o_ref[...] = y.astype(o_ref.dtype)
    else:
        y_ref, = res_ref
        rows = y_ref.shape[1] // dilation
        for c in range(y_ref.shape[0]):
            lanes = slice(c * LANES, (c + 1) * LANES)
            y_ref[c] = y[:, lanes]
            for r in range(dilation):
                o_ref[0, r, :, lanes] = y_ref[c, pl.ds(r, rows, stride=dilation), :].astype(o_ref.dtype)


def _col_tiles(*pieces):
    tiles = []
    for start, width in pieces:
        assert start % PROJ_COLS == 0 and width % PROJ_COLS == 0
        tiles += list(range(start // PROJ_COLS, (start + width) // PROJ_COLS))
    return np.asarray(tiles, np.int32)


def project(x, w, col_tiles, seq_len, dilation=1):
    m, k = x.shape
    tm = min(PROJ_ROWS, seq_len)
    tn = PROJ_COLS
    n = tn * len(col_tiles)
    assert m % seq_len == 0 and seq_len % tm == 0 and w.shape[1] % tn == 0 and tm % (dilation * 16) == 0
    if dilation == 1:
        out_spec = pl.BlockSpec((tm, tn), lambda i, j, tiles: (i, j))
        out_shape = jax.ShapeDtypeStruct((m, n), BF16)
        scratch = []
    else:
        per = seq_len // tm
        out_spec = pl.BlockSpec((1, dilation, tm // dilation, tn), lambda i, j, tiles: (i // per, 0, i % per, j))
        out_shape = jax.ShapeDtypeStruct((m // seq_len, dilation, seq_len // dilation, n), BF16)
        scratch = [pltpu.VMEM((tn // LANES, tm, LANES), F32)]
    return pl.pallas_call(
        functools.partial(_proj_kernel, dilation=dilation),
        grid_spec=pltpu.PrefetchScalarGridSpec(
            num_scalar_prefetch=1,
            grid=(m // tm, len(col_tiles)),
            in_specs=[
                pl.BlockSpec((tm, k), lambda i, j, tiles: (i, 0)),
                pl.BlockSpec((k, tn), lambda i, j, tiles: (0, tiles[j])),
            ],
            out_specs=out_spec,
            scratch_shapes=[pltpu.VMEM((tm, k), BF16)] + scratch,
        ),
        out_shape=out_shape,
        compiler_params=pltpu.CompilerParams(
            dimension_semantics=("parallel", "arbitrary"),
            vmem_limit_bytes=VMEM_LIMIT_BYTES),
        name="project",
    )(jnp.asarray(col_tiles), x, w)


OUT_ROWS = 512
OUT_K = 1024


def _out_ln_kernel(a_ref, a2_ref, w_ref, x_ref, g_ref, b_ref, o_ref, *, n_first):
    kk = pl.program_id(1)

    @pl.when(kk == 0)
    def _():
        o_ref[...] = jnp.dot(a_ref[...], w_ref[...], preferred_element_type=F32)

    @pl.when((kk > 0) & (kk < n_first))
    def _():
        o_ref[...] += jnp.dot(a_ref[...], w_ref[...], preferred_element_type=F32)

    @pl.when(kk >= n_first)
    def _():
        o_ref[...] += jnp.dot(a2_ref[...], w_ref[...], preferred_element_type=F32)

    @pl.when(kk == pl.num_programs(1) - 1)
    def _():
        def store(rows, z):
            o_ref[rows, :] = z

        _layer_norm_loop(x_ref.shape[0], lambda rows: ALPHA * x_ref[rows, :] + o_ref[rows, :], store,
                         g_ref[...], b_ref[...])


def out_ln(a, a2, w, x, g, b):
    m, k1 = a.shape
    k2 = a2.shape[1]
    d = w.shape[1]
    tm = min(OUT_ROWS, m)
    tk = min(OUT_K, k1, k2)
    assert m % tm == 0 and k1 % tk == 0 and k2 % tk == 0 and w.shape[0] == k1 + k2
    n1, n2 = k1 // tk, k2 // tk
    return pl.pallas_call(
        functools.partial(_out_ln_kernel, n_first=n1),
        grid=(m // tm, n1 + n2),
        in_specs=[
            pl.BlockSpec((tm, tk), lambda i, j: (i, jnp.minimum(j, n1 - 1))),
            pl.BlockSpec((tm, tk), lambda i, j: (i, jnp.maximum(j - n1, 0))),
            pl.BlockSpec((tk, d), lambda i, j: (j, 0)),
            pl.BlockSpec((tm, d), lambda i, j: (i, 0)),
            pl.BlockSpec((1, d), lambda i, j: (0, 0)),
            pl.BlockSpec((1, d), lambda i, j: (0, 0)),
        ],
        out_specs=pl.BlockSpec((tm, d), lambda i, j: (i, 0)),
        out_shape=jax.ShapeDtypeStruct((m, d), F32),
        compiler_params=pltpu.CompilerParams(
            dimension_semantics=("parallel", "arbitrary"),
            vmem_limit_bytes=VMEM_LIMIT_BYTES),
        name="out_ln",
    )(a, a2, w, x, g, b)


def _t5_thresholds():
    d = np.arange(0, 2 * REL_MAX_DIST)
    max_exact = REL_BUCKETS // 2
    nf = np.maximum(d, 1).astype(np.float32)
    large = max_exact + (np.log(nf / np.float32(max_exact)) / np.float32(math.log(REL_MAX_DIST / max_exact))
                         * np.float32(REL_BUCKETS - max_exact)).astype(np.int32)
    large = np.minimum(large, REL_BUCKETS - 1)
    bucket = np.where(d < max_exact, d, large)
    return [int(np.argmax(bucket >= k)) for k in range(REL_BUCKETS)]


T5_THR = _t5_thresholds()


def _t5_bias_tile(relb_ref, head, dist):
    acc = jnp.full(dist.shape, relb_ref[0, head], F32)
    for k in range(1, REL_BUCKETS):
        acc = jnp.where(dist >= T5_THR[k], relb_ref[k, head], acc)
    return acc


DSA_TQ = 128
DSA_TS = 512
DSA_HG = 2
DSA_BITS_ALWAYS = 20
DSA_BIT_STAGES = (DSA_BITS_ALWAYS, 22, 24, 26, 28, 31)
SUB = 128
assert T5_THR[REL_BUCKETS - 1] <= SUB


def _ckv_kernel(c_ref, g_ref, cn_ref, ct_ref):
    c = c_ref[0].astype(F32)
    y = c * lax.rsqrt(jnp.mean(c * c, axis=-1, keepdims=True) + LN_EPS) * g_ref[...]
    cn_ref[0] = y.astype(BF16)
    ct_ref[0, 0] = y.T.astype(BF16)


def ckv_norm(u, g, col_block, ts):
    B, S, _ = u.shape
    R = KV_RANK
    return pl.pallas_call(
        _ckv_kernel,
        grid=(B, S // ts),
        in_specs=[pl.BlockSpec((1, ts, R), lambda b, j: (b, j, col_block)),
                  pl.BlockSpec((1, R), lambda b, j: (0, 0))],
        out_specs=[pl.BlockSpec((1, ts, R), lambda b, j: (b, j, 0)),
                   pl.BlockSpec((1, 1, R, ts), lambda b, j: (b, j, 0, 0))],
        out_shape=[jax.ShapeDtypeStruct((B, S, R), BF16),
                   jax.ShapeDtypeStruct((B, S // ts, R, ts), BF16)],
        compiler_params=pltpu.CompilerParams(dimension_semantics=("parallel", "parallel")),
        name="ckv_norm",
    )(u, g)


def _dsa_kernel(relb_ref, qi_ref, q_ref, wi_ref, k_ref, c_ref, ct_ref, wuk_ref, wuvt_ref, o_ref,
                qm_ref, qlt_ref, wt_ref, key_ref, acc_ref, m_ref, l_ref, bias_ref, p_ref, a_ref, *, n_keep):
    b = pl.program_id(0)
    qt = pl.program_id(1)
    S = k_ref.shape[1]
    TS = ct_ref.shape[3]
    TQ = DSA_TQ
    H = DSA_HEADS
    n_sub = TS // SUB
    n_s = (qt * TQ + TQ + TS - 1) // TS
    idx_bits = int(S).bit_length()

    @pl.when((b == 0) & (qt == 0))
    def _():
        s_l = lax.broadcasted_iota(I32, (SUB, TQ), 0)
        t_l = lax.broadcasted_iota(I32, (SUB, TQ), 1)
        for which in range(2):
            for h in range(H):
                bias_ref[which, h] = (_t5_bias_tile(relb_ref, h, t_l - s_l + SUB * which)
                                      - relb_ref[REL_BUCKETS - 1, h])

    for h in range(IDX_HEADS):
        qm_ref[h * TQ:(h + 1) * TQ, :] = qi_ref[0, :, h * IDX_DIM:(h + 1) * IDX_DIM]
    for h in range(H):
        ql = _nt_dot(wuk_ref[:, h * HEAD_DIM:(h + 1) * HEAD_DIM], q_ref[0, :, h * HEAD_DIM:(h + 1) * HEAD_DIM])
        qlt_ref[:, h * TQ:(h + 1) * TQ] = (ql * (HEAD_DIM ** -0.5)).astype(BF16)
    wt_ref[...] = wi_ref[0].astype(F32).T * (IDX_HEADS ** -0.5 * IDX_DIM ** -0.5)

    t_pos = qt * TQ + lax.broadcasted_iota(I32, (TS, TQ), 1)
    s_loc = lax.broadcasted_iota(I32, (TS, TQ), 0)

    def idx_body(j, carry):
        row0 = pl.multiple_of(j * TS, TS)
        kt = k_ref[0, pl.ds(row0, TS), :]
        score = jnp.zeros((TS, TQ), F32)
        for hp in range(IDX_HEADS // 2):
            sc = _nt_dot(kt, qm_ref[hp * 2 * TQ:(hp + 1) * 2 * TQ, :])
            for i in range(2):
                h = 2 * hp + i
                score = score + jnp.maximum(sc[:, i * TQ:(i + 1) * TQ], 0.0) * wt_ref[h:h + 1, :]
        bits = pltpu.bitcast(score, I32)
        key = bits ^ ((bits >> 31) & 0x7FFFFFFF)
        key = jnp.where(row0 + s_loc <= t_pos, key, INT_MIN)
        key_ref[pl.ds(row0, TS), :] = key
        return carry

    lax.fori_loop(0, n_s, idx_body, 0)

    def count(pred_fn):
        def body(j, part):
            row0 = pl.multiple_of(j * TS, TS)
            blk = key_ref[pl.ds(row0, TS), :]
            hit = pred_fn(blk, row0).astype(I32)
            return part + hit.reshape(TS // 8, 8, TQ).sum(axis=0)
        part = lax.fori_loop(0, n_s, body, jnp.zeros((8, TQ), I32))
        return part.sum(axis=0, keepdims=True)

    c0 = count(lambda blk, row0: blk >= 0)
    nonneg = c0 >= n_keep
    thr = jnp.where(nonneg, jnp.zeros((1, TQ), I32), jnp.full((1, TQ), INT_MIN, I32))
    c_ge = jnp.where(nonneg, c0, n_s * TS)

    def bit_body(i, carry):
        thr, c_ge = carry
        cand = thr | (jnp.int32(1) << (30 - i))
        c = count(lambda blk, row0: blk >= cand)
        take = c >= n_keep
        return jnp.where(take, cand, thr), jnp.where(take, c, c_ge)

    def bits(lo, hi, carry):
        exact = jnp.min(jnp.where(carry[1] == n_keep, 1, 0)) == 1
        return lax.cond(exact, lambda: carry, lambda: lax.fori_loop(lo, hi, bit_body, carry))

    carry = lax.fori_loop(0, DSA_BITS_ALWAYS, bit_body, (thr, c_ge))
    for lo, hi in zip(DSA_BIT_STAGES[:-1], DSA_BIT_STAGES[1:]):
        carry = bits(lo, hi, carry)
    thr, c_ge = carry

    def tie_search():
        need = n_keep - count(lambda blk, row0: blk > thr)

        def tie_body(i, x):
            cand = x | (jnp.int32(1) << (idx_bits - 1 - i))
            c = count(lambda blk, row0: (blk == thr) & (row0 + s_loc < cand))
            return jnp.where(c < need, cand, x)
        return lax.fori_loop(0, idx_bits, tie_body, jnp.zeros((1, TQ), I32))

    tie_pos = lax.cond(jnp.max(c_ge) > n_keep, tie_search, lambda: jnp.full((1, TQ), 2 ** idx_bits - 1, I32))

    m_ref[...] = jnp.full(m_ref.shape, NEG_INF, F32)
    l_ref[...] = jnp.zeros(l_ref.shape, F32)
    acc_ref[...] = jnp.zeros(acc_ref.shape, F32)

    def apply_pending(j_prev, pend):
        ckv_t = ct_ref[0, j_prev]
        for hg in range(H // DSA_HG):
            gcols = slice(hg * DSA_HG * TQ, (hg + 1) * DSA_HG * TQ)
            pv = jnp.dot(ckv_t, p_ref[pend, :, gcols], preferred_element_type=F32)
            acc_ref[:, gcols] = acc_ref[:, gcols] * a_ref[pend, :, gcols] + pv

    def att_tile(j, near, pend):
        row0 = pl.multiple_of(j * TS, TS)
        key = key_ref[pl.ds(row0, TS), :]
        s_pos = row0 + s_loc
        sel = ((key > thr) | ((key == thr) & (s_pos <= tie_pos))) & (s_pos <= t_pos)
        ckv = c_ref[0, pl.ds(row0, TS), :]
        for hg in range(H // DSA_HG):
            gcols = slice(hg * DSA_HG * TQ, (hg + 1) * DSA_HG * TQ)
            lg = jnp.dot(ckv, qlt_ref[:, gcols], preferred_element_type=F32)
            for i in range(DSA_HG):
                h = DSA_HG * hg + i
                cols = slice(h * TQ, (h + 1) * TQ)
                x = lg[:, i * TQ:(i + 1) * TQ]
                if near:
                    pieces = []
                    for sb in range(n_sub):
                        delta = qt - (j * n_sub + sb)
                        pieces.append(jnp.where(delta == 0, bias_ref[0, h],
                                                jnp.where(delta == 1, bias_ref[1, h], 0.0)))
                    x = x + (pieces[0] if n_sub == 1 else jnp.concatenate(pieces, axis=0))
                lh = jnp.where(sel, x, NEG_INF)
                m_old = m_ref[:, cols]
                m_new = jnp.maximum(m_old, jnp.max(lh, axis=0, keepdims=True))
                p = jnp.exp(lh - m_new)
                alpha = jnp.exp(m_old - m_new)
                l_ref[:, cols] = alpha * l_ref[:, cols] + jnp.sum(p, axis=0, keepdims=True)
                m_ref[:, cols] = m_new
                p_ref[1 - pend, :, cols] = p.astype(BF16)
                a_ref[1 - pend, :, cols] = alpha
        apply_pending(jnp.maximum(j - 1, 0), pend)

    p_ref[1] = jnp.zeros(p_ref.shape[1:], BF16)
    a_ref[1] = jnp.ones(a_ref.shape[1:], F32)

    def pair_body(near):
        def body(i, carry):
            att_tile(2 * i, near, pend=1)
            att_tile(2 * i + 1, near, pend=0)
            return carry
        return body

    n_far_pairs = (jnp.maximum(qt - 1, 0) // n_sub) // 2
    n_pairs = n_s // 2
    lax.fori_loop(0, n_far_pairs, pair_body(False), 0)
    lax.fori_loop(n_far_pairs, n_pairs, pair_body(True), 0)

    @pl.when(n_s % 2 == 1)
    def _():
        att_tile(n_s - 1, True, pend=1)
        apply_pending(n_s - 1, 0)

    @pl.when(n_s % 2 == 0)
    def _():
        apply_pending(n_s - 1, 1)

    for h in range(H):
        cols = slice(h * TQ, (h + 1) * TQ)
        ol_t = (acc_ref[:, cols] * (1.0 / l_ref[:, cols])).astype(BF16)
        o_t = jnp.dot(wuvt_ref[h], ol_t, preferred_element_type=F32)
        o_ref[0, :, h * HEAD_DIM:(h + 1) * HEAD_DIM] = o_t.T.astype(o_ref.dtype)


def dsa_attention(u, ckv_n, ckv_t, w_uk2, w_uvt, rel_bias, cols, n_keep):
    B, S, _ = u.shape
    ts = ckv_t.shape[3]
    H, R = DSA_HEADS, KV_RANK
    qi_blk, q_blk, wi_blk, k_blk = cols
    kern = functools.partial(_dsa_kernel, n_keep=n_keep)
    return pl.pallas_call(
        kern,
        grid=(B, S // DSA_TQ),
        in_specs=[
            pl.BlockSpec(memory_space=pltpu.SMEM),
            pl.BlockSpec((1, DSA_TQ, IDX_HEADS * IDX_DIM), lambda b, t: (b, t, qi_blk)),
            pl.BlockSpec((1, DSA_TQ, H * HEAD_DIM), lambda b, t: (b, t, q_blk)),
            pl.BlockSpec((1, DSA_TQ, LANES), lambda b, t: (b, t, wi_blk)),
            pl.BlockSpec((1, S, IDX_DIM), lambda b, t: (b, 0, k_blk)),
            pl.BlockSpec((1, S, R), lambda b, t: (b, 0, 0)),
            pl.BlockSpec((1, S // ts, R, ts), lambda b, t: (b, 0, 0, 0)),
            pl.BlockSpec((R, H * HEAD_DIM), lambda b, t: (0, 0)),
            pl.BlockSpec((H, HEAD_DIM, R), lambda b, t: (0, 0, 0)),
        ],
        out_specs=pl.BlockSpec((1, DSA_TQ, H * HEAD_DIM), lambda b, t: (b, t, 0)),
        out_shape=jax.ShapeDtypeStruct((B, S, H * HEAD_DIM), BF16),
        scratch_shapes=[
            pltpu.VMEM((IDX_HEADS * DSA_TQ, IDX_DIM), BF16),
            pltpu.VMEM((R, H * DSA_TQ), BF16),
            pltpu.VMEM((LANES, DSA_TQ), F32),
            pltpu.VMEM((S, DSA_TQ), I32),
            pltpu.VMEM((R, H * DSA_TQ), F32),
            pltpu.VMEM((1, H * DSA_TQ), F32),
            pltpu.VMEM((1, H * DSA_TQ), F32),
            pltpu.VMEM((2, H, SUB, DSA_TQ), F32),
            pltpu.VMEM((2, ts, H * DSA_TQ), BF16),
            pltpu.VMEM((2, 1, H * DSA_TQ), F32),
        ],
        compiler_params=pltpu.CompilerParams(
            dimension_semantics=("arbitrary", "arbitrary"),
            vmem_limit_bytes=VMEM_LIMIT_BYTES),
        name="dsa_attention",
    )(rel_bias, u, u, u, u, ckv_n, ckv_t, w_uk2, w_uvt)


CONV_ROWS = 256
CONV_HALO = 32
CONV_RC = 64
CONV_CW = 512


def _conv_taps(ext_ref, w_ref, y_ref, width, init_fn):
    ts, ch = y_ref.shape
    cw = min(CONV_CW, ch)
    n = CONV_RC + SUBLANES
    assert CONV_HALO >= SUBLANES * ((width - 1) // SUBLANES + 1)

    def body(i, carry):
        r0 = pl.multiple_of(i * CONV_RC, CONV_RC)
        for cc in range(ch // cw):
            cols = slice(cc * cw, (cc + 1) * cw)
            acc = init_fn(cols)
            for b in range(min(SUBLANES, width)):
                z = None
                for a in range((width - b + SUBLANES - 1) // SUBLANES):
                    j = SUBLANES * a + b
                    win = ext_ref[pl.ds(r0 + (CONV_HALO - SUBLANES * (a + 1)), n), cols]
                    term = win * w_ref[width - 1 - j:width - j, cols]
                    z = term if z is None else z + term
                if b == 0:
                    acc = acc + z[SUBLANES:, :]
                else:
                    acc = acc + pltpu.roll(z, n - (SUBLANES - b), axis=0)[:CONV_RC, :]
            y_ref[pl.ds(r0, CONV_RC), cols] = acc
        return carry

    lax.fori_loop(0, ts // CONV_RC, body, 0)


def _conv_gated_kernel(v_ref, g_ref, vp_ref, gp_ref, w_ref, cb_ref, ng_ref, nb_ref, o_ref, ext_ref, y_ref):
    t = pl.program_id(1)
    ts = v_ref.shape[1]
    width = w_ref.shape[0]
    prev = vp_ref[0].astype(F32) * jax.nn.sigmoid(gp_ref[0].astype(F32))
    ext_ref[0:CONV_HALO, :] = jnp.where(t > 0, prev, 0.0)

    def glu_body(i, carry):
        rows = pl.ds(pl.multiple_of(i * CONV_RC, CONV_RC), CONV_RC)
        ext_ref[pl.ds(pl.multiple_of(CONV_HALO + i * CONV_RC, CONV_HALO), CONV_RC), :] = (
            v_ref[0, rows, :].astype(F32) * jax.nn.sigmoid(g_ref[0, rows, :].astype(F32)))
        return carry

    lax.fori_loop(0, ts // CONV_RC, glu_body, 0)
    _conv_taps(ext_ref, w_ref, y_ref, width,
               lambda cols: jnp.broadcast_to(cb_ref[:, cols], (CONV_RC, cols.stop - cols.start)))
    def store(rows, z):
        o_ref[0, rows, :] = (z * jax.nn.sigmoid(z)).astype(o_ref.dtype)

    _layer_norm_loop(ts, lambda rows: y_ref[rows, :], store, ng_ref[...], nb_ref[...])


def _conv_short_kernel(gb_ref, gc_ref, h_ref, gcp_ref, hp_ref, w_ref, o_ref, ext_ref, y_ref):
    t = pl.program_id(1)
    ts = gb_ref.shape[1]
    width = w_ref.shape[0]
    prev = gcp_ref[0].astype(F32) * hp_ref[0].astype(F32)
    ext_ref[0:CONV_HALO, :] = jnp.where(t > 0, prev, 0.0)

    def mul_body(i, carry):
        rows = pl.ds(pl.multiple_of(i * CONV_RC, CONV_RC), CONV_RC)
        ext_ref[pl.ds(pl.multiple_of(CONV_HALO + i * CONV_RC, CONV_HALO), CONV_RC), :] = (
            gc_ref[0, rows, :].astype(F32) * h_ref[0, rows, :].astype(F32))
        return carry

    lax.fori_loop(0, ts // CONV_RC, mul_body, 0)
    _conv_taps(ext_ref, w_ref, y_ref, width,
               lambda cols: jnp.zeros((CONV_RC, cols.stop - cols.start), F32))

    def out_body(i, carry):
        rows = pl.ds(pl.multiple_of(i * CONV_RC, CONV_RC), CONV_RC)
        o_ref[0, rows, :] = (gb_ref[0, rows, :].astype(F32) * y_ref[rows, :]).astype(o_ref.dtype)
        return carry

    lax.fori_loop(0, ts // CONV_RC, out_body, 0)


def _halo_spec(ch, ts, col_block):
    per = ts // CONV_HALO
    return pl.BlockSpec((1, CONV_HALO, ch), lambda b, t: (b, jnp.maximum(t * per - 1, 0), col_block))


def conv_gated(u, val_blk, gate_blk, ch, conv_w, conv_b, norm_g, norm_b):
    B, S, _ = u.shape
    ts = min(CONV_ROWS, S)
    assert conv_w.shape[0] - 1 <= CONV_HALO and S % ts == 0
    cur = lambda blk: pl.BlockSpec((1, ts, ch), lambda b, t: (b, t, blk))
    vec = pl.BlockSpec((1, ch), lambda b, t: (0, 0))
    return pl.pallas_call(
        _conv_gated_kernel,
        grid=(B, S // ts),
        in_specs=[cur(val_blk), cur(gate_blk), _halo_spec(ch, ts, val_blk), _halo_spec(ch, ts, gate_blk),
                  pl.BlockSpec(conv_w.shape, lambda b, t: (0, 0)), vec, vec, vec],
        out_specs=pl.BlockSpec((1, ts, ch), lambda b, t: (b, t, 0)),
        out_shape=jax.ShapeDtypeStruct((B, S, ch), BF16),
        scratch_shapes=[pltpu.VMEM((CONV_HALO + ts, ch), F32), pltpu.VMEM((ts, ch), F32)],
        compiler_params=pltpu.CompilerParams(dimension_semantics=("parallel", "parallel"),
                                             vmem_limit_bytes=VMEM_LIMIT_BYTES),
        name="conv_gated",
    )(u, u, u, u, conv_w, conv_b.reshape(1, ch), norm_g.reshape(1, ch), norm_b.reshape(1, ch))


def conv_short(u, gb_blk, gc_blk, h_blk, ch, conv_w):
    B, S, _ = u.shape
    ts = min(CONV_ROWS, S)
    assert conv_w.shape[0] - 1 <= CONV_HALO and S % ts == 0
    cur = lambda blk: pl.BlockSpec((1, ts, ch), lambda b, t: (b, t, blk))
    return pl.pallas_call(
        _conv_short_kernel,
        grid=(B, S // ts),
        in_specs=[cur(gb_blk), cur(gc_blk), cur(h_blk), _halo_spec(ch, ts, gc_blk), _halo_spec(ch, ts, h_blk),
                  pl.BlockSpec(conv_w.shape, lambda b, t: (0, 0))],
        out_specs=pl.BlockSpec((1, ts, ch), lambda b, t: (b, t, 0)),
        out_shape=jax.ShapeDtypeStruct((B, S, ch), BF16),
        scratch_shapes=[pltpu.VMEM((CONV_HALO + ts, ch), F32), pltpu.VMEM((ts, ch), F32)],
        compiler_params=pltpu.CompilerParams(dimension_semantics=("parallel", "parallel"),
                                             vmem_limit_bytes=VMEM_LIMIT_BYTES),
        name="conv_short",
    )(u, u, u, u, u, conv_w)


def _dilated_kernel(relb_ref, q_ref, kp_ref, kc_ref, vp_ref, vc_ref, o_ref, lse_ref, bias_ref, *, dilation, steps):
    n = pl.program_id(2)
    BLK = DIL_BLOCK
    i_pos = lax.broadcasted_iota(I32, (BLK, 2 * BLK), 0)
    j_pos = lax.broadcasted_iota(I32, (BLK, 2 * BLK), 1)
    m = BLK + i_pos - j_pos

    @pl.when((pl.program_id(0) == 0) & (pl.program_id(1) == 0) & (n == 0))
    def _():
        for h in range(DIL_SLOTS):
            bias_ref[h] = _t5_bias_tile(relb_ref, h, m * dilation)

    valid = (m >= 0) & (m <= steps) & ((n > 0) | (j_pos >= BLK))
    for h in range(DIL_SLOTS):
        cols = slice(h * HEAD_DIM, (h + 1) * HEAD_DIM)
        q = q_ref[0, 0, :, cols]
        kb = jnp.concatenate([kp_ref[0, 0, :, cols], kc_ref[0, 0, :, cols]], axis=0)
        vb = jnp.concatenate([vp_ref[0, 0, :, cols], vc_ref[0, 0, :, cols]], axis=0)
        lg = _nt_dot(q, kb) * (HEAD_DIM ** -0.5) + bias_ref[h]
        lg = jnp.where(valid, lg, NEG_INF)
        mx = jnp.max(lg, axis=-1, keepdims=True)
        p = jnp.exp(lg - mx)
        l = jnp.sum(p, axis=-1, keepdims=True)
        o = jnp.dot(p.astype(BF16), vb, preferred_element_type=F32)
        o_ref[0, 0, :, cols] = o * (1.0 / l)
        lse_ref[0, 0, :, cols] = jnp.broadcast_to(mx + jnp.log(l), (BLK, HEAD_DIM))


def dilated_group(qkv, rel_bias, q_blk, k_blk, v_blk, window):
    B, dilation, L, N = qkv.shape
    W = DIL_SLOTS * HEAD_DIM
    assert L % DIL_BLOCK == 0 and N % W == 0 and window // dilation <= DIL_BLOCK
    cur = lambda blk: pl.BlockSpec((1, 1, DIL_BLOCK, W), lambda b, r, n: (b, r, n, blk))
    prev = lambda blk: pl.BlockSpec((1, 1, DIL_BLOCK, W), lambda b, r, n: (b, r, jnp.maximum(n - 1, 0), blk))
    out = pl.BlockSpec((1, 1, DIL_BLOCK, W), lambda b, r, n: (b, r, n, 0))
    kern = functools.partial(_dilated_kernel, dilation=dilation, steps=window // dilation)
    return pl.pallas_call(
        kern,
        grid=(B, dilation, L // DIL_BLOCK),
        in_specs=[pl.BlockSpec(memory_space=pltpu.SMEM), cur(q_blk), prev(k_blk), cur(k_blk), prev(v_blk), cur(v_blk)],
        out_specs=[out, out],
        out_shape=[jax.ShapeDtypeStruct((B, dilation, L, W), F32)] * 2,
        scratch_shapes=[pltpu.VMEM((DIL_SLOTS, DIL_BLOCK, 2 * DIL_BLOCK), F32)],
        compiler_params=pltpu.CompilerParams(dimension_semantics=("arbitrary", "arbitrary", "arbitrary")),
        name=f"dilated_d{dilation}",
    )(rel_bias, qkv, qkv, qkv, qkv, qkv)


def _dil_combine_kernel(*refs, dilations):
    g = len(dilations)
    o_refs, l_refs, out_ref, tmp_refs = refs[:g], refs[g:2 * g], refs[2 * g], refs[2 * g + 1:]

    def in_position_order(ref, tmp_ref, d):
        if d == 1:
            return ref[0, 0]
        rows = ref.shape[2]
        for c in range(tmp_ref.shape[0]):
            for r in range(d):
                tmp_ref[c, pl.ds(r, rows, stride=d), :] = ref[0, r, :, c * LANES:(c + 1) * LANES]
        return jnp.concatenate([tmp_ref[c] for c in range(tmp_ref.shape[0])], axis=1)

    ls = [in_position_order(l_refs[i], tmp_refs[2 * i], d) for i, d in enumerate(dilations)]
    mx = functools.reduce(jnp.maximum, ls)
    es = [jnp.exp(l - mx) for l in ls]
    den = functools.reduce(jnp.add, es)
    num = None
    for i, d in enumerate(dilations):
        term = es[i] * in_position_order(o_refs[i], tmp_refs[2 * i + 1], d)
        num = term if num is None else num + term
    out_ref[0] = (num / den).astype(out_ref.dtype)


DIL_COMBINE_ROWS = 512


def dilated_combine(outs, lses):
    B, _, _, W = outs[0].shape
    dilations = tuple(o.shape[1] for o in outs)
    S = outs[0].shape[1] * outs[0].shape[2]
    ts = min(DIL_COMBINE_ROWS, S)
    assert all(ts % (d * SUBLANES) == 0 for d in dilations)
    specs = [pl.BlockSpec((1, d, ts // d, W), lambda b, t: (b, 0, t, 0)) for d in dilations]
    return pl.pallas_call(
        functools.partial(_dil_combine_kernel, dilations=dilations),
        grid=(B, S // ts),
        in_specs=specs + specs,
        out_specs=pl.BlockSpec((1, ts, W), lambda b, t: (b, t, 0)),
        out_shape=jax.ShapeDtypeStruct((B, S, W), BF16),
        scratch_shapes=[pltpu.VMEM((W // LANES, ts, LANES), F32)] * (2 * len(dilations)),
        compiler_params=pltpu.CompilerParams(dimension_semantics=("parallel", "parallel"),
                                             vmem_limit_bytes=VMEM_LIMIT_BYTES),
        name="dilated_combine",
    )(*outs, *lses)


EVEN_ATTN_COLS = 6144
EVEN_QIDX_COL = 0
EVEN_Q_COL = IDX_HEADS * IDX_DIM
EVEN_CKV_COL = EVEN_Q_COL + DSA_HEADS * HEAD_DIM
EVEN_KIDX_COL = EVEN_CKV_COL + KV_RANK
EVEN_WIDX_COL = EVEN_KIDX_COL + IDX_DIM


def _even_col_tiles(ca):
    q0 = 2 * ca
    ckv0 = q0 + DSA_HEADS * HEAD_DIM
    qidx0 = ckv0 + KV_RANK
    kidx0 = qidx0 + IDX_HEADS * IDX_DIM
    tiles = _col_tiles((qidx0, IDX_HEADS * IDX_DIM), (q0, DSA_HEADS * HEAD_DIM), (ckv0, KV_RANK),
                       (kidx0, PROJ_COLS), (0, 2 * ca))
    assert len(tiles) * PROJ_COLS == EVEN_ATTN_COLS + 2 * ca
    return tiles


def mixer_even(xs, w_in, conv_w, conv_b, norm_g, norm_b, kv_norm_g, w_uk, w_uv, rel_bias, B, S):
    ca = conv_w.shape[-1]
    assert EVEN_ATTN_COLS % ca == 0
    w_b = w_in.astype(BF16)
    w_b = jnp.pad(w_b, ((0, 0), (0, (-w_b.shape[1]) % PROJ_COLS)))
    u = project(xs, w_b, _even_col_tiles(ca), S).reshape(B, S, -1)
    a = conv_gated(u, EVEN_ATTN_COLS // ca, EVEN_ATTN_COLS // ca + 1, ca, conv_w, conv_b, norm_g, norm_b)
    cn, ct = ckv_norm(u, kv_norm_g.reshape(1, KV_RANK), EVEN_CKV_COL // KV_RANK, min(DSA_TS, S))
    o = dsa_attention(u, cn, ct, w_uk.reshape(KV_RANK, DSA_HEADS * HEAD_DIM).astype(BF16),
                      jnp.transpose(w_uv, (1, 2, 0)).astype(BF16), rel_bias,
                      (EVEN_QIDX_COL // (IDX_HEADS * IDX_DIM), EVEN_Q_COL // (DSA_HEADS * HEAD_DIM),
                       EVEN_WIDX_COL // LANES, EVEN_KIDX_COL // IDX_DIM), min(IDX_TOPK, S // 4))
    return a, o


def mixer_odd(xs, w_in, conv_w, rel_bias, B, S):
    cc = conv_w.shape[-1]
    W = DIL_SLOTS * HEAD_DIM
    assert (3 * cc) % W == 0 and DIL_CONFIGS[0][1] == 1
    w_b = w_in.astype(BF16)
    qkv_tiles = lambda g: [(3 * cc + (i * DIL_GROUPS + g) * W, W) for i in range(3)]
    u = project(xs, w_b, _col_tiles((0, 3 * cc), *qkv_tiles(0)), S).reshape(B, S, -1)
    c_out = conv_short(u, 0, 1, 2, cc, conv_w)
    q_blk = 3 * cc // W
    results = [dilated_group(u.reshape(B, 1, S, -1), rel_bias, q_blk, q_blk + 1, q_blk + 2, DIL_CONFIGS[0][0])]
    for g, (window, dilation) in list(enumerate(DIL_CONFIGS))[1:]:
        qkv = project(xs, w_b, _col_tiles(*qkv_tiles(g)), S, dilation)
        results.append(dilated_group(qkv, rel_bias, 0, 1, 2, window))
    o = dilated_combine([r[0] for r in results], [r[1] for r in results])
    return c_out, o


def kernel(x, w_ffn_in, w_ffn_out, post_ln_g, post_ln_b, rel_bias, w_in_e, w_out_e, conv_a_w, conv_a_b,
           conv_a_norm_g, conv_a_norm_b, kv_norm_g, w_uk, w_uv, w_in_o, w_out_o, conv_c_w):
    B, S, D = x.shape
    M = B * S
    depth = w_ffn_in.shape[0]
    assert depth == DEPTH
    xs = x.reshape(M, D)
    w_ffn_in_b = w_ffn_in.astype(BF16)
    w_ffn_out_b = w_ffn_out.astype(BF16)
    for layer in range(depth):
        g = post_ln_g[layer].reshape(3, 1, D)
        b = post_ln_b[layer].reshape(3, 1, D)
        xs = ffn_ln(xs, w_ffn_in_b, w_ffn_out_b, layer, 0, g[0], b[0])
        if layer % 2 == 0:
            e = layer // 2
            conv_out, attn_out = mixer_even(xs, w_in_e[e], conv_a_w[e], conv_a_b[e], conv_a_norm_g[e],
                                            conv_a_norm_b[e], kv_norm_g[e], w_uk[e], w_uv[e], rel_bias, B, S)
            w_out = w_out_e[e]
        else:
            o = layer // 2
            conv_out, attn_out = mixer_odd(xs, w_in_o[o], conv_c_w[o], rel_bias, B, S)
            w_out = w_out_o[o]
        xs = out_ln(conv_out.reshape(M, -1), attn_out.reshape(M, -1), w_out.astype(BF16), xs, g[1], b[1])
        xs = ffn_ln(xs, w_ffn_in_b, w_ffn_out_b, layer, 1, g[2], b[2])
    return xs.reshape(B, S, D)
```

```python
import functools
import math

import jax
import jax.numpy as jnp
import numpy as np
from jax import lax
from jax.experimental import pallas as pl
from jax.experimental.pallas import tpu as pltpu

DEPTH = 2
ALPHA = (2 * DEPTH) ** 0.25
LN_EPS = 1e-5
NEG_INF = -1e30

HEAD_DIM = 128
DSA_HEADS = 8
KV_RANK = 512
IDX_HEADS = 32
IDX_DIM = 128
IDX_TOPK = 256
DIL_SLOTS = 8
DIL_CONFIGS = ((128, 1), (512, 4), (2048, 16))
DIL_GROUPS = len(DIL_CONFIGS)
DIL_HEADS = DIL_SLOTS * DIL_GROUPS
DIL_BLOCK = 128
REL_BUCKETS = 32
REL_MAX_DIST = 128

V7X_VMEM_BYTES = 64 * 1024 * 1024
VMEM_LIMIT_BYTES = V7X_VMEM_BYTES - 6 * 1024 * 1024

BF16 = jnp.bfloat16
F32 = jnp.float32
I32 = jnp.int32
INT_MIN = -2 ** 31
SUBLANES = 8
LANES = 128


def _layer_norm_rows(y, g, b):
    mu = jnp.mean(y, axis=-1, keepdims=True)
    yc = y - mu
    var = jnp.mean(yc * yc, axis=-1, keepdims=True)
    return yc * lax.rsqrt(var + LN_EPS) * g + b


LN_CHUNK = 16
LN_GROUP = 4


def _layer_norm_loop(n_rows, load_fn, store_fn, g, b):
    step = LN_CHUNK * LN_GROUP
    assert n_rows % step == 0

    def body(i, carry):
        base = pl.multiple_of(i * step, step)
        rows = [pl.ds(base + c * LN_CHUNK, LN_CHUNK) for c in range(LN_GROUP)]
        ys = [load_fn(r) for r in rows]
        for r, y in zip(rows, ys):
            store_fn(r, _layer_norm_rows(y, g, b))
        return carry

    lax.fori_loop(0, n_rows // step, body, 0)


def _nt_dot(a, b):
    return lax.dot_general(a, b, (((1,), (1,)), ((), ())), preferred_element_type=F32)


FFN_ROWS = 512
FFN_HIDDEN = 256


def _ffn_ln_kernel(x_ref, wg_ref, wu_ref, wo_ref, g_ref, b_ref, o_ref, xb_ref):
    f = pl.program_id(1)

    @pl.when(f == 0)
    def _():
        xb_ref[...] = x_ref[...].astype(BF16)
        o_ref[...] = jnp.zeros_like(o_ref)

    xb = xb_ref[...]
    gate = jnp.dot(xb, wg_ref[...], preferred_element_type=F32)
    up = jnp.dot(xb, wu_ref[...], preferred_element_type=F32)
    h = (gate * jax.nn.sigmoid(gate) * up).astype(BF16)
    o_ref[...] += jnp.dot(h, wo_ref[...], preferred_element_type=F32)

    @pl.when(f == pl.num_programs(1) - 1)
    def _():
        def store(rows, z):
            o_ref[rows, :] = z

        _layer_norm_loop(x_ref.shape[0], lambda rows: ALPHA * x_ref[rows, :] + 0.5 * o_ref[rows, :], store,
                         g_ref[...], b_ref[...])


def ffn_ln(x, w_in, w_out, layer, idx, g, b):
    m, d = x.shape
    f = w_out.shape[2]
    tm = min(FFN_ROWS, m)
    tf = FFN_HIDDEN
    assert m % tm == 0 and f % tf == 0
    nf = f // tf
    return pl.pallas_call(
        _ffn_ln_kernel,
        grid=(m // tm, nf),
        in_specs=[
            pl.BlockSpec((tm, d), lambda i, j: (i, 0)),
            pl.BlockSpec((None, None, d, tf), lambda i, j: (layer, idx, 0, j)),
            pl.BlockSpec((None, None, d, tf), lambda i, j: (layer, idx, 0, j + nf)),
            pl.BlockSpec((None, None, tf, d), lambda i, j: (layer, idx, j, 0)),
            pl.BlockSpec((1, d), lambda i, j: (0, 0)),
            pl.BlockSpec((1, d), lambda i, j: (0, 0)),
        ],
        out_specs=pl.BlockSpec((tm, d), lambda i, j: (i, 0)),
        out_shape=jax.ShapeDtypeStruct((m, d), F32),
        scratch_shapes=[pltpu.VMEM((tm, d), BF16)],
        compiler_params=pltpu.CompilerParams(
            dimension_semantics=("parallel", "arbitrary"),
            vmem_limit_bytes=VMEM_LIMIT_BYTES),
        name="ffn_ln",
    )(x, w_in, w_in, w_out, g, b)


PROJ_ROWS = 1024
PROJ_COLS = 512


def _proj_kernel(tiles_ref, x_ref, w_ref, o_ref, xb_ref, *res_ref, dilation):
    del tiles_ref

    @pl.when(pl.program_id(1) == 0)
    def _():
        xb_ref[...] = x_ref[...].astype(BF16)

    y = jnp.dot(xb_ref[...], w_ref[...], preferred_element_type=F32)
    if dilation == 1:
        o_ref[...] = y.astype(o_ref.dtype)
    else:
        y_ref, = res_ref
        rows = y_ref.shape[1] // dilation
        for c in range(y_ref.shape[0]):
            lanes = slice(c * LANES, (c + 1) * LANES)
            y_ref[c] = y[:, lanes]
            for r in range(dilation):
                o_ref[0, r, :, lanes] = y_ref[c, pl.ds(r, rows, stride=dilation), :].astype(o_ref.dtype)


def _col_tiles(*pieces):
    tiles = []
    for start, width in pieces:
        assert start % PROJ_COLS == 0 and width % PROJ_COLS == 0
        tiles += list(range(start // PROJ_COLS, (start + width) // PROJ_COLS))
    return np.asarray(tiles, np.int32)


def project(x, w, col_tiles, seq_len, dilation=1):
    m, k = x.shape
    tm = min(PROJ_ROWS, seq_len)
    tn = PROJ_COLS
    n = tn * len(col_tiles)
    assert m % seq_len == 0 and seq_len % tm == 0 and w.shape[1] % tn == 0 and tm % (dilation * 16) == 0
    if dilation == 1:
        out_spec = pl.BlockSpec((tm, tn), lambda i, j, tiles: (i, j))
        out_shape = jax.ShapeDtypeStruct((m, n), BF16)
        scratch = []
    else:
        per = seq_len // tm
        out_spec = pl.BlockSpec((1, dilation, tm // dilation, tn), lambda i, j, tiles: (i // per, 0, i % per, j))
        out_shape = jax.ShapeDtypeStruct((m // seq_len, dilation, seq_len // dilation, n), BF16)
        scratch = [pltpu.VMEM((tn // LANES, tm, LANES), F32)]
    return pl.pallas_call(
        functools.partial(_proj_kernel, dilation=dilation),
        grid_spec=pltpu.PrefetchScalarGridSpec(
            num_scalar_prefetch=1,
            grid=(m // tm, len(col_tiles)),
            in_specs=[
                pl.BlockSpec((tm, k), lambda i, j, tiles: (i, 0)),
                pl.BlockSpec((k, tn), lambda i, j, tiles: (0, tiles[j])),
            ],
            out_specs=out_spec,
            scratch_shapes=[pltpu.VMEM((tm, k), BF16)] + scratch,
        ),
        out_shape=out_shape,
        compiler_params=pltpu.CompilerParams(
            dimension_semantics=("parallel", "arbitrary"),
            vmem_limit_bytes=VMEM_LIMIT_BYTES),
        name="project",
    )(jnp.asarray(col_tiles), x, w)


OUT_ROWS = 512
OUT_K = 1024


def _out_ln_kernel(a_ref, a2_ref, w_ref, x_ref, g_ref, b_ref, o_ref, *, n_first):
    kk = pl.program_id(1)

    @pl.when(kk == 0)
    def _():
        o_ref[...] = jnp.dot(a_ref[...], w_ref[...], preferred_element_type=F32)

    @pl.when((kk > 0) & (kk < n_first))
    def _():
        o_ref[...] += jnp.dot(a_ref[...], w_ref[...], preferred_element_type=F32)

    @pl.when(kk >= n_first)
    def _():
        o_ref[...] += jnp.dot(a2_ref[...], w_ref[...], preferred_element_type=F32)

    @pl.when(kk == pl.num_programs(1) - 1)
    def _():
        def store(rows, z):
            o_ref[rows, :] = z

        _layer_norm_loop(x_ref.shape[0], lambda rows: ALPHA * x_ref[rows, :] + o_ref[rows, :], store,
                         g_ref[...], b_ref[...])


def out_ln(a, a2, w, x, g, b):
    m, k1 = a.shape
    k2 = a2.shape[1]
    d = w.shape[1]
    tm = min(OUT_ROWS, m)
    tk = min(OUT_K, k1, k2)
    assert m % tm == 0 and k1 % tk == 0 and k2 % tk == 0 and w.shape[0] == k1 + k2
    n1, n2 = k1 // tk, k2 // tk
    return pl.pallas_call(
        functools.partial(_out_ln_kernel, n_first=n1),
        grid=(m // tm, n1 + n2),
        in_specs=[
            pl.BlockSpec((tm, tk), lambda i, j: (i, jnp.minimum(j, n1 - 1))),
            pl.BlockSpec((tm, tk), lambda i, j: (i, jnp.maximum(j - n1, 0))),
            pl.BlockSpec((tk, d), lambda i, j: (j, 0)),
            pl.BlockSpec((tm, d), lambda i, j: (i, 0)),
            pl.BlockSpec((1, d), lambda i, j: (0, 0)),
            pl.BlockSpec((1, d), lambda i, j: (0, 0)),
        ],
        out_specs=pl.BlockSpec((tm, d), lambda i, j: (i, 0)),
        out_shape=jax.ShapeDtypeStruct((m, d), F32),
        compiler_params=pltpu.CompilerParams(
            dimension_semantics=("parallel", "arbitrary"),
            vmem_limit_bytes=VMEM_LIMIT_BYTES),
        name="out_ln",
    )(a, a2, w, x, g, b)


def _t5_thresholds():
    d = np.arange(0, 2 * REL_MAX_DIST)
    max_exact = REL_BUCKETS // 2
    nf = np.maximum(d, 1).astype(np.float32)
    large = max_exact + (np.log(nf / np.float32(max_exact)) / np.float32(math.log(REL_MAX_DIST / max_exact))
                         * np.float32(REL_BUCKETS - max_exact)).astype(np.int32)
    large = np.minimum(large, REL_BUCKETS - 1)
    bucket = np.where(d < max_exact, d, large)
    return [int(np.argmax(bucket >= k)) for k in range(REL_BUCKETS)]


T5_THR = _t5_thresholds()


def _t5_bias_tile(relb_ref, head, dist):
    acc = jnp.full(dist.shape, relb_ref[0, head], F32)
    for k in range(1, REL_BUCKETS):
        acc = jnp.where(dist >= T5_THR[k], relb_ref[k, head], acc)
    return acc


DSA_TQ = 128
DSA_TS = 512
DSA_HG = 2
DSA_BITS_ALWAYS = 20
DSA_BIT_STAGES = (DSA_BITS_ALWAYS, 22, 24, 26, 28, 31)
SUB = 128
assert T5_THR[REL_BUCKETS - 1] <= SUB


def _ckv_kernel(c_ref, g_ref, cn_ref, ct_ref):
    c = c_ref[0].astype(F32)
    y = c * lax.rsqrt(jnp.mean(c * c, axis=-1, keepdims=True) + LN_EPS) * g_ref[...]
    cn_ref[0] = y.astype(BF16)
    ct_ref[0, 0] = y.T.astype(BF16)


def ckv_norm(u, g, col_block, ts):
    B, S, _ = u.shape
    R = KV_RANK
    return pl.pallas_call(
        _ckv_kernel,
        grid=(B, S // ts),
        in_specs=[pl.BlockSpec((1, ts, R), lambda b, j: (b, j, col_block)),
                  pl.BlockSpec((1, R), lambda b, j: (0, 0))],
        out_specs=[pl.BlockSpec((1, ts, R), lambda b, j: (b, j, 0)),
                   pl.BlockSpec((1, 1, R, ts), lambda b, j: (b, j, 0, 0))],
        out_shape=[jax.ShapeDtypeStruct((B, S, R), BF16),
                   jax.ShapeDtypeStruct((B, S // ts, R, ts), BF16)],
        compiler_params=pltpu.CompilerParams(dimension_semantics=("parallel", "parallel")),
        name="ckv_norm",
    )(u, g)


def _dsa_kernel(relb_ref, qi_ref, q_ref, wi_ref, k_ref, c_ref, ct_ref, wuk_ref, wuvt_ref, o_ref,
                qm_ref, qlt_ref, wt_ref, key_ref, acc_ref, m_ref, l_ref, bias_ref, p_ref, a_ref, *, n_keep):
    b = pl.program_id(0)
    qt = pl.program_id(1)
    S = k_ref.shape[1]
    TS = ct_ref.shape[3]
    TQ = DSA_TQ
    H = DSA_HEADS
    n_sub = TS // SUB
    n_s = (qt * TQ + TQ + TS - 1) // TS
    idx_bits = int(S).bit_length()

    @pl.when((b == 0) & (qt == 0))
    def _():
        s_l = lax.broadcasted_iota(I32, (SUB, TQ), 0)
        t_l = lax.broadcasted_iota(I32, (SUB, TQ), 1)
        for which in range(2):
            for h in range(H):
                bias_ref[which, h] = (_t5_bias_tile(relb_ref, h, t_l - s_l + SUB * which)
                                      - relb_ref[REL_BUCKETS - 1, h])

    for h in range(IDX_HEADS):
        qm_ref[h * TQ:(h + 1) * TQ, :] = qi_ref[0, :, h * IDX_DIM:(h + 1) * IDX_DIM]
    for h in range(H):
        ql = _nt_dot(wuk_ref[:, h * HEAD_DIM:(h + 1) * HEAD_DIM], q_ref[0, :, h * HEAD_DIM:(h + 1) * HEAD_DIM])
        qlt_ref[:, h * TQ:(h + 1) * TQ] = (ql * (HEAD_DIM ** -0.5)).astype(BF16)
    wt_ref[...] = wi_ref[0].astype(F32).T * (IDX_HEADS ** -0.5 * IDX_DIM ** -0.5)

    t_pos = qt * TQ + lax.broadcasted_iota(I32, (TS, TQ), 1)
    s_loc = lax.broadcasted_iota(I32, (TS, TQ), 0)

    def idx_body(j, carry):
        row0 = pl.multiple_of(j * TS, TS)
        kt = k_ref[0, pl.ds(row0, TS), :]
        score = jnp.zeros((TS, TQ), F32)
        for hp in range(IDX_HEADS // 2):
            sc = _nt_dot(kt, qm_ref[hp * 2 * TQ:(hp + 1) * 2 * TQ, :])
            for i in range(2):
                h = 2 * hp + i
                score = score + jnp.maximum(sc[:, i * TQ:(i + 1) * TQ], 0.0) * wt_ref[h:h + 1, :]
        bits = pltpu.bitcast(score, I32)
        key = bits ^ ((bits >> 31) & 0x7FFFFFFF)
        key = jnp.where(row0 + s_loc <= t_pos, key, INT_MIN)
        key_ref[pl.ds(row0, TS), :] = key
        return carry

    lax.fori_loop(0, n_s, idx_body, 0)

    def count(pred_fn):
        def body(j, part):
            row0 = pl.multiple_of(j * TS, TS)
            blk = key_ref[pl.ds(row0, TS), :]
            hit = pred_fn(blk, row0).astype(I32)
            return part + hit.reshape(TS // 8, 8, TQ).sum(axis=0)
        part = lax.fori_loop(0, n_s, body, jnp.zeros((8, TQ), I32))
        return part.sum(axis=0, keepdims=True)

    c0 = count(lambda blk, row0: blk >= 0)
    nonneg = c0 >= n_keep
    thr = jnp.where(nonneg, jnp.zeros((1, TQ), I32), jnp.full((1, TQ), INT_MIN, I32))
    c_ge = jnp.where(nonneg, c0, n_s * TS)

    def bit_body(i, carry):
        thr, c_ge = carry
        cand = thr | (jnp.int32(1) << (30 - i))
        c = count(lambda blk, row0: blk >= cand)
        take = c >= n_keep
        return jnp.where(take, cand, thr), jnp.where(take, c, c_ge)

    def bits(lo, hi, carry):
        exact = jnp.min(jnp.where(carry[1] == n_keep, 1, 0)) == 1
        return lax.cond(exact, lambda: carry, lambda: lax.fori_loop(lo, hi, bit_body, carry))

    carry = lax.fori_loop(0, DSA_BITS_ALWAYS, bit_body, (thr, c_ge))
    for lo, hi in zip(DSA_BIT_STAGES[:-1], DSA_BIT_STAGES[1:]):
        carry = bits(lo, hi, carry)
    thr, c_ge = carry

    def tie_search():
        need = n_keep - count(lambda blk, row0: blk > thr)

        def tie_body(i, x):
            cand = x | (jnp.int32(1) << (idx_bits - 1 - i))
            c = count(lambda blk, row0: (blk == thr) & (row0 + s_loc < cand))
            return jnp.where(c < need, cand, x)
        return lax.fori_loop(0, idx_bits, tie_body, jnp.zeros((1, TQ), I32))

    tie_pos = lax.cond(jnp.max(c_ge) > n_keep, tie_search, lambda: jnp.full((1, TQ), 2 ** idx_bits - 1, I32))

    m_ref[...] = jnp.full(m_ref.shape, NEG_INF, F32)
    l_ref[...] = jnp.zeros(l_ref.shape, F32)
    acc_ref[...] = jnp.zeros(acc_ref.shape, F32)

    def apply_pending(j_prev, pend):
        ckv_t = ct_ref[0, j_prev]
        for hg in range(H // DSA_HG):
            gcols = slice(hg * DSA_HG * TQ, (hg + 1) * DSA_HG * TQ)
            pv = jnp.dot(ckv_t, p_ref[pend, :, gcols], preferred_element_type=F32)
            acc_ref[:, gcols] = acc_ref[:, gcols] * a_ref[pend, :, gcols] + pv

    def att_tile(j, near, pend):
        row0 = pl.multiple_of(j * TS, TS)
        key = key_ref[pl.ds(row0, TS), :]
        s_pos = row0 + s_loc
        sel = ((key > thr) | ((key == thr) & (s_pos <= tie_pos))) & (s_pos <= t_pos)
        ckv = c_ref[0, pl.ds(row0, TS), :]
        for hg in range(H // DSA_HG):
            gcols = slice(hg * DSA_HG * TQ, (hg + 1) * DSA_HG * TQ)
            lg = jnp.dot(ckv, qlt_ref[:, gcols], preferred_element_type=F32)
            for i in range(DSA_HG):
                h = DSA_HG * hg + i
                cols = slice(h * TQ, (h + 1) * TQ)
                x = lg[:, i * TQ:(i + 1) * TQ]
                if near:
                    pieces = []
                    for sb in range(n_sub):
                        delta = qt - (j * n_sub + sb)
                        pieces.append(jnp.where(delta == 0, bias_ref[0, h],
                                                jnp.where(delta == 1, bias_ref[1, h], 0.0)))
                    x = x + (pieces[0] if n_sub == 1 else jnp.concatenate(pieces, axis=0))
                lh = jnp.where(sel, x, NEG_INF)
                m_old = m_ref[:, cols]
                m_new = jnp.maximum(m_old, jnp.max(lh, axis=0, keepdims=True))
                p = jnp.exp(lh - m_new)
                alpha = jnp.exp(m_old - m_new)
                l_ref[:, cols] = alpha * l_ref[:, cols] + jnp.sum(p, axis=0, keepdims=True)
                m_ref[:, cols] = m_new
                p_ref[1 - pend, :, cols] = p.astype(BF16)
                a_ref[1 - pend, :, cols] = alpha
        apply_pending(jnp.maximum(j - 1, 0), pend)

    p_ref[1] = jnp.zeros(p_ref.shape[1:], BF16)
    a_ref[1] = jnp.ones(a_ref.shape[1:], F32)

    def pair_body(near):
        def body(i, carry):
            att_tile(2 * i, near, pend=1)
            att_tile(2 * i + 1, near, pend=0)
            return carry
        return body

    n_far_pairs = (jnp.maximum(qt - 1, 0) // n_sub) // 2
    n_pairs = n_s // 2
    lax.fori_loop(0, n_far_pairs, pair_body(False), 0)
    lax.fori_loop(n_far_pairs, n_pairs, pair_body(True), 0)

    @pl.when(n_s % 2 == 1)
    def _():
        att_tile(n_s - 1, True, pend=1)
        apply_pending(n_s - 1, 0)

    @pl.when(n_s % 2 == 0)
    def _():
        apply_pending(n_s - 1, 1)

    for h in range(H):
        cols = slice(h * TQ, (h + 1) * TQ)
        ol_t = (acc_ref[:, cols] * (1.0 / l_ref[:, cols])).astype(BF16)
        o_t = jnp.dot(wuvt_ref[h], ol_t, preferred_element_type=F32)
        o_ref[0, :, h * HEAD_DIM:(h + 1) * HEAD_DIM] = o_t.T.astype(o_ref.dtype)


def dsa_attention(u, ckv_n, ckv_t, w_uk2, w_uvt, rel_bias, cols, n_keep):
    B, S, _ = u.shape
    ts = ckv_t.shape[3]
    H, R = DSA_HEADS, KV_RANK
    qi_blk, q_blk, wi_blk, k_blk = cols
    kern = functools.partial(_dsa_kernel, n_keep=n_keep)
    return pl.pallas_call(
        kern,
        grid=(B, S // DSA_TQ),
        in_specs=[
            pl.BlockSpec(memory_space=pltpu.SMEM),
            pl.BlockSpec((1, DSA_TQ, IDX_HEADS * IDX_DIM), lambda b, t: (b, t, qi_blk)),
            pl.BlockSpec((1, DSA_TQ, H * HEAD_DIM), lambda b, t: (b, t, q_blk)),
            pl.BlockSpec((1, DSA_TQ, LANES), lambda b, t: (b, t, wi_blk)),
            pl.BlockSpec((1, S, IDX_DIM), lambda b, t: (b, 0, k_blk)),
            pl.BlockSpec((1, S, R), lambda b, t: (b, 0, 0)),
            pl.BlockSpec((1, S // ts, R, ts), lambda b, t: (b, 0, 0, 0)),
            pl.BlockSpec((R, H * HEAD_DIM), lambda b, t: (0, 0)),
            pl.BlockSpec((H, HEAD_DIM, R), lambda b, t: (0, 0, 0)),
        ],
        out_specs=pl.BlockSpec((1, DSA_TQ, H * HEAD_DIM), lambda b, t: (b, t, 0)),
        out_shape=jax.ShapeDtypeStruct((B, S, H * HEAD_DIM), BF16),
        scratch_shapes=[
            pltpu.VMEM((IDX_HEADS * DSA_TQ, IDX_DIM), BF16),
            pltpu.VMEM((R, H * DSA_TQ), BF16),
            pltpu.VMEM((LANES, DSA_TQ), F32),
            pltpu.VMEM((S, DSA_TQ), I32),
            pltpu.VMEM((R, H * DSA_TQ), F32),
            pltpu.VMEM((1, H * DSA_TQ), F32),
            pltpu.VMEM((1, H * DSA_TQ), F32),
            pltpu.VMEM((2, H, SUB, DSA_TQ), F32),
            pltpu.VMEM((2, ts, H * DSA_TQ), BF16),
            pltpu.VMEM((2, 1, H * DSA_TQ), F32),
        ],
        compiler_params=pltpu.CompilerParams(
            dimension_semantics=("arbitrary", "arbitrary"),
            vmem_limit_bytes=VMEM_LIMIT_BYTES),
        name="dsa_attention",
    )(rel_bias, u, u, u, u, ckv_n, ckv_t, w_uk2, w_uvt)


CONV_ROWS = 256
CONV_HALO = 32
CONV_RC = 64
CONV_CW = 512


def _conv_taps(ext_ref, w_ref, y_ref, width, init_fn):
    ts, ch = y_ref.shape
    cw = min(CONV_CW, ch)
    n = CONV_RC + SUBLANES
    assert CONV_HALO >= SUBLANES * ((width - 1) // SUBLANES + 1)

    def body(i, carry):
        r0 = pl.multiple_of(i * CONV_RC, CONV_RC)
        for cc in range(ch // cw):
            cols = slice(cc * cw, (cc + 1) * cw)
            acc = init_fn(cols)
            for b in range(min(SUBLANES, width)):
                z = None
                for a in range((width - b + SUBLANES - 1) // SUBLANES):
                    j = SUBLANES * a + b
                    win = ext_ref[pl.ds(r0 + (CONV_HALO - SUBLANES * (a + 1)), n), cols]
                    term = win * w_ref[width - 1 - j:width - j, cols]
                    z = term if z is None else z + term
                if b == 0:
                    acc = acc + z[SUBLANES:, :]
                else:
                    acc = acc + pltpu.roll(z, n - (SUBLANES - b), axis=0)[:CONV_RC, :]
            y_ref[pl.ds(r0, CONV_RC), cols] = acc
        return carry

    lax.fori_loop(0, ts // CONV_RC, body, 0)


def _conv_gated_kernel(v_ref, g_ref, vp_ref, gp_ref, w_ref, cb_ref, ng_ref, nb_ref, o_ref, ext_ref, y_ref):
    t = pl.program_id(1)
    ts = v_ref.shape[1]
    width = w_ref.shape[0]
    prev = vp_ref[0].astype(F32) * jax.nn.sigmoid(gp_ref[0].astype(F32))
    ext_ref[0:CONV_HALO, :] = jnp.where(t > 0, prev, 0.0)

    def glu_body(i, carry):
        rows = pl.ds(pl.multiple_of(i * CONV_RC, CONV_RC), CONV_RC)
        ext_ref[pl.ds(pl.multiple_of(CONV_HALO + i * CONV_RC, CONV_HALO), CONV_RC), :] = (
            v_ref[0, rows, :].astype(F32) * jax.nn.sigmoid(g_ref[0, rows, :].astype(F32)))
        return carry

    lax.fori_loop(0, ts // CONV_RC, glu_body, 0)
    _conv_taps(ext_ref, w_ref, y_ref, width,
               lambda cols: jnp.broadcast_to(cb_ref[:, cols], (CONV_RC, cols.stop - cols.start)))
    def store(rows, z):
        o_ref[0, rows, :] = (z * jax.nn.sigmoid(z)).astype(o_ref.dtype)

    _layer_norm_loop(ts, lambda rows: y_ref[rows, :], store, ng_ref[...], nb_ref[...])


def _conv_short_kernel(gb_ref, gc_ref, h_ref, gcp_ref, hp_ref, w_ref, o_ref, ext_ref, y_ref):
    t = pl.program_id(1)
    ts = gb_ref.shape[1]
    width = w_ref.shape[0]
    prev = gcp_ref[0].astype(F32) * hp_ref[0].astype(F32)
    ext_ref[0:CONV_HALO, :] = jnp.where(t > 0, prev, 0.0)

    def mul_body(i, carry):
        rows = pl.ds(pl.multiple_of(i * CONV_RC, CONV_RC), CONV_RC)
        ext_ref[pl.ds(pl.multiple_of(CONV_HALO + i * CONV_RC, CONV_HALO), CONV_RC), :] = (
            gc_ref[0, rows, :].astype(F32) * h_ref[0, rows, :].astype(F32))
        return carry

    lax.fori_loop(0, ts // CONV_RC, mul_body, 0)
    _conv_taps(ext_ref, w_ref, y_ref, width,
               lambda cols: jnp.zeros((CONV_RC, cols.stop - cols.start), F32))

    def out_body(i, carry):
        rows = pl.ds(pl.multiple_of(i * CONV_RC, CONV_RC), CONV_RC)
        o_ref[0, rows, :] = (gb_ref[0, rows, :].astype(F32) * y_ref[rows, :]).astype(o_ref.dtype)
        return carry

    lax.fori_loop(0, ts // CONV_RC, out_body, 0)


def _halo_spec(ch, ts, col_block):
    per = ts // CONV_HALO
    return pl.BlockSpec((1, CONV_HALO, ch), lambda b, t: (b, jnp.maximum(t * per - 1, 0), col_block))


def conv_gated(u, val_blk, gate_blk, ch, conv_w, conv_b, norm_g, norm_b):
    B, S, _ = u.shape
    ts = min(CONV_ROWS, S)
    assert conv_w.shape[0] - 1 <= CONV_HALO and S % ts == 0
    cur = lambda blk: pl.BlockSpec((1, ts, ch), lambda b, t: (b, t, blk))
    vec = pl.BlockSpec((1, ch), lambda b, t: (0, 0))
    return pl.pallas_call(
        _conv_gated_kernel,
        grid=(B, S // ts),
        in_specs=[cur(val_blk), cur(gate_blk), _halo_spec(ch, ts, val_blk), _halo_spec(ch, ts, gate_blk),
                  pl.BlockSpec(conv_w.shape, lambda b, t: (0, 0)), vec, vec, vec],
        out_specs=pl.BlockSpec((1, ts, ch), lambda b, t: (b, t, 0)),
        out_shape=jax.ShapeDtypeStruct((B, S, ch), BF16),
        scratch_shapes=[pltpu.VMEM((CONV_HALO + ts, ch), F32), pltpu.VMEM((ts, ch), F32)],
        compiler_params=pltpu.CompilerParams(dimension_semantics=("parallel", "parallel"),
                                             vmem_limit_bytes=VMEM_LIMIT_BYTES),
        name="conv_gated",
    )(u, u, u, u, conv_w, conv_b.reshape(1, ch), norm_g.reshape(1, ch), norm_b.reshape(1, ch))


def conv_short(u, gb_blk, gc_blk, h_blk, ch, conv_w):
    B, S, _ = u.shape
    ts = min(CONV_ROWS, S)
    assert conv_w.shape[0] - 1 <= CONV_HALO and S % ts == 0
    cur = lambda blk: pl.BlockSpec((1, ts, ch), lambda b, t: (b, t, blk))
    return pl.pallas_call(
        _conv_short_kernel,
        grid=(B, S // ts),
        in_specs=[cur(gb_blk), cur(gc_blk), cur(h_blk), _halo_spec(ch, ts, gc_blk), _halo_spec(ch, ts, h_blk),
                  pl.BlockSpec(conv_w.shape, lambda b, t: (0, 0))],
        out_specs=pl.BlockSpec((1, ts, ch), lambda b, t: (b, t, 0)),
        out_shape=jax.ShapeDtypeStruct((B, S, ch), BF16),
        scratch_shapes=[pltpu.VMEM((CONV_HALO + ts, ch), F32), pltpu.VMEM((ts, ch), F32)],
        compiler_params=pltpu.CompilerParams(dimension_semantics=("parallel", "parallel"),
                                             vmem_limit_bytes=VMEM_LIMIT_BYTES),
        name="conv_short",
    )(u, u, u, u, u, conv_w)


def _dilated_kernel(relb_ref, q_ref, kp_ref, kc_ref, vp_ref, vc_ref, o_ref, lse_ref, bias_ref, *, dilation, steps):
    n = pl.program_id(2)
    BLK = DIL_BLOCK
    i_pos = lax.broadcasted_iota(I32, (BLK, 2 * BLK), 0)
    j_pos = lax.broadcasted_iota(I32, (BLK, 2 * BLK), 1)
    m = BLK + i_pos - j_pos

    @pl.when((pl.program_id(0) == 0) & (pl.program_id(1) == 0) & (n == 0))
    def _():
        for h in range(DIL_SLOTS):
            bias_ref[h] = _t5_bias_tile(relb_ref, h, m * dilation)

    valid = (m >= 0) & (m <= steps) & ((n > 0) | (j_pos >= BLK))
    for h in range(DIL_SLOTS):
        cols = slice(h * HEAD_DIM, (h + 1) * HEAD_DIM)
        q = q_ref[0, 0, :, cols]
        kb = jnp.concatenate([kp_ref[0, 0, :, cols], kc_ref[0, 0, :, cols]], axis=0)
        vb = jnp.concatenate([vp_ref[0, 0, :, cols], vc_ref[0, 0, :, cols]], axis=0)
        lg = _nt_dot(q, kb) * (HEAD_DIM ** -0.5) + bias_ref[h]
        lg = jnp.where(valid, lg, NEG_INF)
        mx = jnp.max(lg, axis=-1, keepdims=True)
        p = jnp.exp(lg - mx)
        l = jnp.sum(p, axis=-1, keepdims=True)
        o = jnp.dot(p.astype(BF16), vb, preferred_element_type=F32)
        o_ref[0, 0, :, cols] = o * (1.0 / l)
        lse_ref[0, 0, :, cols] = jnp.broadcast_to(mx + jnp.log(l), (BLK, HEAD_DIM))


def dilated_group(qkv, rel_bias, q_blk, k_blk, v_blk, window):
    B, dilation, L, N = qkv.shape
    W = DIL_SLOTS * HEAD_DIM
    assert L % DIL_BLOCK == 0 and N % W == 0 and window // dilation <= DIL_BLOCK
    cur = lambda blk: pl.BlockSpec((1, 1, DIL_BLOCK, W), lambda b, r, n: (b, r, n, blk))
    prev = lambda blk: pl.BlockSpec((1, 1, DIL_BLOCK, W), lambda b, r, n: (b, r, jnp.maximum(n - 1, 0), blk))
    out = pl.BlockSpec((1, 1, DIL_BLOCK, W), lambda b, r, n: (b, r, n, 0))
    kern = functools.partial(_dilated_kernel, dilation=dilation, steps=window // dilation)
    return pl.pallas_call(
        kern,
        grid=(B, dilation, L // DIL_BLOCK),
        in_specs=[pl.BlockSpec(memory_space=pltpu.SMEM), cur(q_blk), prev(k_blk), cur(k_blk), prev(v_blk), cur(v_blk)],
        out_specs=[out, out],
        out_shape=[jax.ShapeDtypeStruct((B, dilation, L, W), F32)] * 2,
        scratch_shapes=[pltpu.VMEM((DIL_SLOTS, DIL_BLOCK, 2 * DIL_BLOCK), F32)],
        compiler_params=pltpu.CompilerParams(dimension_semantics=("arbitrary", "arbitrary", "arbitrary")),
        name=f"dilated_d{dilation}",
    )(rel_bias, qkv, qkv, qkv, qkv, qkv)


def _dil_combine_kernel(*refs, dilations):
    g = len(dilations)
    o_refs, l_refs, out_ref, tmp_refs = refs[:g], refs[g:2 * g], refs[2 * g], refs[2 * g + 1:]

    def in_position_order(ref, tmp_ref, d):
        if d == 1:
            return ref[0, 0]
        rows = ref.shape[2]
        for c in range(tmp_ref.shape[0]):
            for r in range(d):
                tmp_ref[c, pl.ds(r, rows, stride=d), :] = ref[0, r, :, c * LANES:(c + 1) * LANES]
        return jnp.concatenate([tmp_ref[c] for c in range(tmp_ref.shape[0])], axis=1)

    ls = [in_position_order(l_refs[i], tmp_refs[2 * i], d) for i, d in enumerate(dilations)]
    mx = functools.reduce(jnp.maximum, ls)
    es = [jnp.exp(l - mx) for l in ls]
    den = functools.reduce(jnp.add, es)
    num = None
    for i, d in enumerate(dilations):
        term = es[i] * in_position_order(o_refs[i], tmp_refs[2 * i + 1], d)
        num = term if num is None else num + term
    out_ref[0] = (num / den).astype(out_ref.dtype)


DIL_COMBINE_ROWS = 512


def dilated_combine(outs, lses):
    B, _, _, W = outs[0].shape
    dilations = tuple(o.shape[1] for o in outs)
    S = outs[0].shape[1] * outs[0].shape[2]
    ts = min(DIL_COMBINE_ROWS, S)
    assert all(ts % (d * SUBLANES) == 0 for d in dilations)
    specs = [pl.BlockSpec((1, d, ts // d, W), lambda b, t: (b, 0, t, 0)) for d in dilations]
    return pl.pallas_call(
        functools.partial(_dil_combine_kernel, dilations=dilations),
        grid=(B, S // ts),
        in_specs=specs + specs,
        out_specs=pl.BlockSpec((1, ts, W), lambda b, t: (b, t, 0)),
        out_shape=jax.ShapeDtypeStruct((B, S, W), BF16),
        scratch_shapes=[pltpu.VMEM((W // LANES, ts, LANES), F32)] * (2 * len(dilations)),
        compiler_params=pltpu.CompilerParams(dimension_semantics=("parallel", "parallel"),
                                             vmem_limit_bytes=VMEM_LIMIT_BYTES),
        name="dilated_combine",
    )(*outs, *lses)


EVEN_ATTN_COLS = 6144
EVEN_QIDX_COL = 0
EVEN_Q_COL = IDX_HEADS * IDX_DIM
EVEN_CKV_COL = EVEN_Q_COL + DSA_HEADS * HEAD_DIM
EVEN_KIDX_COL = EVEN_CKV_COL + KV_RANK
EVEN_WIDX_COL = EVEN_KIDX_COL + IDX_DIM


def _even_col_tiles(ca):
    q0 = 2 * ca
    ckv0 = q0 + DSA_HEADS * HEAD_DIM
    qidx0 = ckv0 + KV_RANK
    kidx0 = qidx0 + IDX_HEADS * IDX_DIM
    tiles = _col_tiles((qidx0, IDX_HEADS * IDX_DIM), (q0, DSA_HEADS * HEAD_DIM), (ckv0, KV_RANK),
                       (kidx0, PROJ_COLS), (0, 2 * ca))
    assert len(tiles) * PROJ_COLS == EVEN_ATTN_COLS + 2 * ca
    return tiles


def mixer_even(xs, w_in, conv_w, conv_b, norm_g, norm_b, kv_norm_g, w_uk, w_uv, rel_bias, B, S):
    ca = conv_w.shape[-1]
    assert EVEN_ATTN_COLS % ca == 0
    w_b = w_in.astype(BF16)
    w_b = jnp.pad(w_b, ((0, 0), (0, (-w_b.shape[1]) % PROJ_COLS)))
    u = project(xs, w_b, _even_col_tiles(ca), S).reshape(B, S, -1)
    a = conv_gated(u, EVEN_ATTN_COLS // ca, EVEN_ATTN_COLS // ca + 1, ca, conv_w, conv_b, norm_g, norm_b)
    cn, ct = ckv_norm(u, kv_norm_g.reshape(1, KV_RANK), EVEN_CKV_COL // KV_RANK, min(DSA_TS, S))
    o = dsa_attention(u, cn, ct, w_uk.reshape(KV_RANK, DSA_HEADS * HEAD_DIM).astype(BF16),
                      jnp.transpose(w_uv, (1, 2, 0)).astype(BF16), rel_bias,
                      (EVEN_QIDX_COL // (IDX_HEADS * IDX_DIM), EVEN_Q_COL // (DSA_HEADS * HEAD_DIM),
                       EVEN_WIDX_COL // LANES, EVEN_KIDX_COL // IDX_DIM), min(IDX_TOPK, S // 4))
    return a, o


def mixer_odd(xs, w_in, conv_w, rel_bias, B, S):
    cc = conv_w.shape[-1]
    W = DIL_SLOTS * HEAD_DIM
    assert (3 * cc) % W == 0 and DIL_CONFIGS[0][1] == 1
    w_b = w_in.astype(BF16)
    qkv_tiles = lambda g: [(3 * cc + (i * DIL_GROUPS + g) * W, W) for i in range(3)]
    u = project(xs, w_b, _col_tiles((0, 3 * cc), *qkv_tiles(0)), S).reshape(B, S, -1)
    c_out = conv_short(u, 0, 1, 2, cc, conv_w)
    q_blk = 3 * cc // W
    results = [dilated_group(u.reshape(B, 1, S, -1), rel_bias, q_blk, q_blk + 1, q_blk + 2, DIL_CONFIGS[0][0])]
    for g, (window, dilation) in list(enumerate(DIL_CONFIGS))[1:]:
        qkv = project(xs, w_b, _col_tiles(*qkv_tiles(g)), S, dilation)
        results.append(dilated_group(qkv, rel_bias, 0, 1, 2, window))
    o = dilated_combine([r[0] for r in results], [r[1] for r in results])
    return c_out, o


def kernel(x, w_ffn_in, w_ffn_out, post_ln_g, post_ln_b, rel_bias, w_in_e, w_out_e, conv_a_w, conv_a_b,
           conv_a_norm_g, conv_a_norm_b, kv_norm_g, w_uk, w_uv, w_in_o, w_out_o, conv_c_w):
    B, S, D = x.shape
    M = B * S
    depth = w_ffn_in.shape[0]
    assert depth == DEPTH
    xs = x.reshape(M, D)
    w_ffn_in_b = w_ffn_in.astype(BF16)
    w_ffn_out_b = w_ffn_out.astype(BF16)
    for layer in range(depth):
        g = post_ln_g[layer].reshape(3, 1, D)
        b = post_ln_b[layer].reshape(3, 1, D)
        xs = ffn_ln(xs, w_ffn_in_b, w_ffn_out_b, layer, 0, g[0], b[0])
        if layer % 2 == 0:
            e = layer // 2
            conv_out, attn_out = mixer_even(xs, w_in_e[e], conv_a_w[e], conv_a_b[e], conv_a_norm_g[e],
                                            conv_a_norm_b[e], kv_norm_g[e], w_uk[e], w_uv[e], rel_bias, B, S)
            w_out = w_out_e[e]
        else:
            o = layer // 2
            conv_out, attn_out = mixer_odd(xs, w_in_o[o], conv_c_w[o], rel_bias, B, S)
            w_out = w_out_o[o]
        xs = out_ln(conv_out.reshape(M, -1), attn_out.reshape(M, -1), w_out.astype(BF16), xs, g[1], b[1])
        xs = ffn_ln(xs, w_ffn_in_b, w_ffn_out_b, layer, 1, g[2], b[2])
    return xs.reshape(B, S, D)
```
